```python
import math
import jax, jax.numpy as jnp
from jax import lax
import numpy as np

D_MODEL = 4096
BATCH = 4
SEQ = 2048
DEPTH = 2

CHUNK = 64

D_MIX = D_MODEL
D_GROUP = D_MIX // 4

CONV_K = 31

RWKV_HEAD = 64
RWKV_HEADS = D_GROUP // RWKV_HEAD
DECAY_LORA = 64
ICLR_LORA = 64
GATE_LORA = 160
LNX_EPS = 64e-5

SGU_BLOCK = 128
SGU_HEADS = 8
SGU_HEAD = D_GROUP // SGU_HEADS

SB_HEAD = 128
SB_HEADS = D_GROUP // SB_HEAD
SB_BLOCK = 128

D_FF = 11008
EPS = 1e-6

N_CONV_IN = 2 * D_GROUP
N_RWKV_IN = 3 * D_GROUP + DECAY_LORA + ICLR_LORA + GATE_LORA
N_SGU_IN = 2 * D_GROUP
N_SB_IN = 3 * D_GROUP
N_IN = N_CONV_IN + N_RWKV_IN + N_SGU_IN + N_SB_IN

kernel_name = "hybrid_parallel_conv_rwkv7_sgu_stickbreaking_macaron"


def _rmsnorm(x, g):
    xf = x.astype(jnp.float32)
    y = xf * lax.rsqrt(jnp.mean(xf * xf, axis=-1, keepdims=True) + EPS)
    return (y * g.astype(jnp.float32)).astype(x.dtype)


def _layernorm(x, g, b, eps=EPS):
    xf = x.astype(jnp.float32)
    mu = jnp.mean(xf, axis=-1, keepdims=True)
    var = jnp.mean(jnp.square(xf - mu), axis=-1, keepdims=True)
    y = (xf - mu) * lax.rsqrt(var + eps)
    return (y * g.astype(jnp.float32) + b.astype(jnp.float32)).astype(x.dtype)


def _swiglu(x, w_in, w_out):
    gate, up = jnp.split(x @ w_in, 2, axis=-1)
    return (jax.nn.silu(gate) * up) @ w_out


def _token_shift(h):
    return jnp.pad(h[:, :-1], ((0, 0), (1, 0), (0, 0)))


def _conv_mixer(h, w_dw, b_dw, ln_g, ln_b):
    a, gate = jnp.split(h, 2, axis=-1)
    y = a * jax.nn.sigmoid(gate)
    y = lax.conv_general_dilated(
        y, w_dw[:, None, :].astype(y.dtype), window_strides=(1,),
        padding=[(CONV_K - 1, 0)],
        dimension_numbers=("NWC", "WIO", "NWC"),
        feature_group_count=y.shape[-1]) + b_dw
    y = _layernorm(y, ln_g, ln_b)
    return jax.nn.silu(y)


def _rwkv_step(state, inp):
    r_t, w_t, k_t, v_t, kk_t, a_t = inp
    sa = jnp.einsum('bhvk,bhk->bhv', state, -kk_t)
    state = (state * w_t[:, :, None, :]
             + sa[..., None] * (kk_t * a_t)[:, :, None, :]
             + v_t[..., None] * k_t[:, :, None, :])
    y_t = jnp.einsum('bhvk,bhk->bhv', state, r_t)
    return state, y_t


def _rwkv7_mixer(h, mu, w0, w_up, a0, a_up, g_up, k_k, k_a, r_k, lnx_g, lnx_b):
    B, S, _ = h.shape
    C, H, N = D_GROUP, RWKV_HEADS, RWKV_HEAD
    h = h + (_token_shift(h) - h) * mu
    r, k, v, xw, xa, xg = jnp.split(
        h, [C, 2 * C, 3 * C, 3 * C + DECAY_LORA, 3 * C + DECAY_LORA + ICLR_LORA], axis=-1)
    f32 = jnp.float32
    w_log = -jax.nn.softplus(-(w0 + jnp.tanh(xw) @ w_up).astype(f32)) - 0.5
    decay = jnp.exp(-jnp.exp(w_log))
    a = jax.nn.sigmoid((a0 + xa @ a_up).astype(f32))
    g = jax.nn.sigmoid(xg) @ g_up
    r = r.astype(f32).reshape(B, S, H, N)
    v = v.astype(f32).reshape(B, S, H, N)
    kf = k.astype(f32)
    kk = (kf * k_k.astype(f32)).reshape(B, S, H, N)
    kk = kk / jnp.maximum(jnp.linalg.norm(kk, axis=-1, keepdims=True), 1e-12)
    kf = (kf * (1.0 + (a - 1.0) * k_a.astype(f32))).reshape(B, S, H, N)
    decay = decay.reshape(B, S, H, N)
    a = a.reshape(B, S, H, N)
    xs = tuple(jnp.moveaxis(t, 1, 0) for t in (r, decay, kf, v, kk, a))
    state0 = jnp.zeros((B, H, N, N), f32)
    _, ys = lax.scan(_rwkv_step, state0, xs)
    y = jnp.moveaxis(ys, 0, 1)
    y = _layernorm(y, lnx_g.reshape(H, N), lnx_b.reshape(H, N), eps=LNX_EPS)
    bonus = jnp.sum(r * kf * r_k.astype(f32), axis=-1, keepdims=True) * v
    y = (y + bonus).reshape(B, S, C)
    return (y * g.astype(f32)).astype(h.dtype)


def _sgu_mixer(h, ln_g, ln_b, w_s, b_s):
    B, S, _ = h.shape
    u, v = jnp.split(jax.nn.gelu(h), 2, axis=-1)
    v = _layernorm(v, ln_g, ln_b)
    nb = S // SGU_BLOCK
    v = v.reshape(B, nb, SGU_BLOCK, SGU_HEADS, SGU_HEAD)
    pos = jnp.arange(SGU_BLOCK)
    chunk_causal = (pos[None, :] // CHUNK) <= (pos[:, None] // CHUNK)
    w = jnp.where(chunk_causal[None], w_s, 0.0).astype(v.dtype)
    sv = jnp.einsum('hij,bnjhc->bnihc', w, v) + b_s.T[None, None, :, :, None]
    return u * sv.reshape(B, S, D_GROUP)


def _stick_breaking_mixer(h):
    B, S, _ = h.shape
    q, k, v = jnp.split(h, 3, axis=-1)
    to_heads = lambda t: t.reshape(B, S, SB_HEADS, SB_HEAD).transpose(0, 2, 1, 3)
    q, k, v = to_heads(q), to_heads(k), to_heads(v)
    scale = SB_HEAD ** -0.5
    outs = []
    for blk in range(S // SB_BLOCK):
        q0 = blk * SB_BLOCK
        L = q0 + SB_BLOCK
        z = jnp.einsum('bhqd,bhkd->bhqk', q[:, :, q0:L], k[:, :, :L]).astype(jnp.float32) * scale
        t_idx = q0 + jnp.arange(SB_BLOCK)[:, None]
        s_idx = jnp.arange(L)[None, :]
        before = s_idx < t_idx
        log_keep = jnp.where(before, jax.nn.log_sigmoid(-z), 0.0)
        later = lax.cumsum(log_keep, axis=3, reverse=True) - log_keep
        weights = jnp.where(before, jnp.exp(jax.nn.log_sigmoid(z) + later), 0.0)
        outs.append(jnp.einsum('bhqk,bhkd->bhqd', weights.astype(v.dtype), v[:, :, :L]))
    o = jnp.concatenate(outs, axis=2)
    return o.transpose(0, 2, 1, 3).reshape(B, S, D_GROUP)


def setup_inputs(seed: int = 0) -> dict:
    key = jax.random.key(seed)
    ks = jax.random.split(key, 32)
    f32 = jnp.float32
    nrm = lambda k, shape, s: jax.random.normal(k, shape, f32) * s
    gain = lambda k, shape: 1.0 + 0.01 * jax.random.normal(k, shape, f32)
    return {
        "x": jax.random.normal(ks[0], (BATCH, SEQ, D_MODEL), f32),
        "ffn1_norm": gain(ks[1], (DEPTH, D_MODEL)),
        "ffn1_w_in": nrm(ks[2], (DEPTH, D_MODEL, 2 * D_FF), D_MODEL ** -0.5),
        "ffn1_w_out": nrm(ks[3], (DEPTH, D_FF, D_MODEL), D_FF ** -0.5),
        "mix_norm": gain(ks[4], (DEPTH, D_MODEL)),
        "mix_w_in": nrm(ks[5], (DEPTH, D_MODEL, N_IN), D_MODEL ** -0.5),
        "conv_w": nrm(ks[6], (DEPTH, CONV_K, D_GROUP), CONV_K ** -0.5),
        "conv_b": nrm(ks[7], (DEPTH, D_GROUP), 0.01),
        "conv_ln_g": gain(ks[8], (DEPTH, D_GROUP)),
        "conv_ln_b": nrm(ks[9], (DEPTH, D_GROUP), 0.01),
        "rwkv_mu": jax.random.uniform(ks[10], (DEPTH, N_RWKV_IN), f32),
        "rwkv_w0": jax.random.uniform(ks[11], (DEPTH, D_GROUP), f32, -3.0, 1.0),
        "rwkv_w_up": nrm(ks[12], (DEPTH, DECAY_LORA, D_GROUP), 0.1 * DECAY_LORA ** -0.5),
        "rwkv_a0": nrm(ks[13], (DEPTH, D_GROUP), 0.1),
        "rwkv_a_up": nrm(ks[14], (DEPTH, ICLR_LORA, D_GROUP), 0.1 * ICLR_LORA ** -0.5),
        "rwkv_g_up": nrm(ks[15], (DEPTH, GATE_LORA, D_GROUP), GATE_LORA ** -0.5),
        "rwkv_k_k": 0.85 + nrm(ks[16], (DEPTH, D_GROUP), 0.02),
        "rwkv_k_a": 1.0 + nrm(ks[17], (DEPTH, D_GROUP), 0.02),
        "rwkv_r_k": nrm(ks[18], (DEPTH, RWKV_HEADS, RWKV_HEAD), 0.1),
        "rwkv_lnx_g": gain(ks[19], (DEPTH, D_GROUP)),
        "rwkv_lnx_b": nrm(ks[20], (DEPTH, D_GROUP), 0.01),
        "sgu_ln_g": gain(ks[21], (DEPTH, D_GROUP)),
        "sgu_ln_b": nrm(ks[22], (DEPTH, D_GROUP), 0.01),
        "sgu_w_s": nrm(ks[23], (DEPTH, SGU_HEADS, SGU_BLOCK, SGU_BLOCK), SGU_BLOCK ** -0.5),
        "sgu_b_s": 1.0 + nrm(ks[24], (DEPTH, SGU_HEADS, SGU_BLOCK), 0.01),
        "mix_w_out": nrm(ks[25], (DEPTH, D_MIX, D_MODEL), D_MIX ** -0.5),
        "ffn2_norm": gain(ks[26], (DEPTH, D_MODEL)),
        "ffn2_w_in": nrm(ks[27], (DEPTH, D_MODEL, 2 * D_FF), D_MODEL ** -0.5),
        "ffn2_w_out": nrm(ks[28], (DEPTH, D_FF, D_MODEL), D_FF ** -0.5),
        "final_norm": gain(ks[29], (D_MODEL,)),
    }


def reference(x, ffn1_norm, ffn1_w_in, ffn1_w_out, mix_norm, mix_w_in, conv_w, conv_b,
              conv_ln_g, conv_ln_b, rwkv_mu, rwkv_w0, rwkv_w_up, rwkv_a0, rwkv_a_up,
              rwkv_g_up, rwkv_k_k, rwkv_k_a, rwkv_r_k, rwkv_lnx_g, rwkv_lnx_b,
              sgu_ln_g, sgu_ln_b, sgu_w_s, sgu_b_s, mix_w_out, ffn2_norm, ffn2_w_in,
              ffn2_w_out, final_norm):
    splits = [N_CONV_IN, N_CONV_IN + N_RWKV_IN, N_CONV_IN + N_RWKV_IN + N_SGU_IN]
    for l in range(DEPTH):
        x = x + 0.5 * _swiglu(_rmsnorm(x, ffn1_norm[l]), ffn1_w_in[l], ffn1_w_out[l])
        h = _rmsnorm(x, mix_norm[l]) @ mix_w_in[l]
        h_conv, h_rwkv, h_sgu, h_sb = jnp.split(h, splits, axis=-1)
        y_conv = _conv_mixer(h_conv, conv_w[l], conv_b[l], conv_ln_g[l], conv_ln_b[l])
        y_rwkv = _rwkv7_mixer(h_rwkv, rwkv_mu[l], rwkv_w0[l], rwkv_w_up[l], rwkv_a0[l],
                              rwkv_a_up[l], rwkv_g_up[l], rwkv_k_k[l], rwkv_k_a[l],
                              rwkv_r_k[l], rwkv_lnx_g[l], rwkv_lnx_b[l])
        y_sgu = _sgu_mixer(h_sgu, sgu_ln_g[l], sgu_ln_b[l], sgu_w_s[l], sgu_b_s[l])
        y_sb = _stick_breaking_mixer(h_sb)
        y = jnp.concatenate([y_conv, y_rwkv.astype(y_conv.dtype), y_sgu, y_sb], axis=-1)
        x = x + y @ mix_w_out[l]
        x = x + 0.5 * _swiglu(_rmsnorm(x, ffn2_norm[l]), ffn2_w_in[l], ffn2_w_out[l])
    return _rmsnorm(x, final_norm)
```

```python
import functools

import jax
import jax.numpy as jnp
from jax import lax
from jax.experimental import pallas as pl
from jax.experimental.pallas import tpu as pltpu

LANES = 128
SUBLANES = 8
VMEM_LIMIT_BYTES = 52 * 1024 * 1024

CHUNK = 64
CONV_K = 31
CONV_HALO = 32
RWKV_HEAD = 64
DECAY_LORA = 64
ICLR_LORA = 64
GATE_LORA = 160
LNX_EPS = 64e-5
SGU_BLOCK = 128
SGU_HEAD = 128
SB_HEAD = 128
SB_BLOCK = 128
EPS = 1e-6

BF16 = jnp.bfloat16
F32 = jnp.float32


def _params(*sem):
    return pltpu.CompilerParams(dimension_semantics=sem, vmem_limit_bytes=VMEM_LIMIT_BYTES)


def _dot(a, b):
    return jnp.dot(a, b, preferred_element_type=F32)


def _blk(n, pref):
    if n <= pref:
        return n
    for cand in range(pref - pref % LANES, 0, -LANES):
        if n % cand == 0:
            return cand
    return n


def _rmsnorm_kernel(x_ref, g_ref, o_ref):
    x = x_ref[...]
    ms = jnp.mean(x * x, axis=-1, keepdims=True)
    o_ref[...] = (x * lax.rsqrt(ms + EPS) * g_ref[...]).astype(o_ref.dtype)


def _rmsnorm(x, g, out_dtype, bm=256):
    m, d = x.shape
    bm = _blk(m, bm)
    return pl.pallas_call(
        _rmsnorm_kernel,
        grid=(m // bm,),
        in_specs=[pl.BlockSpec((bm, d), lambda i: (i, 0)),
                  pl.BlockSpec((1, d), lambda i: (0, 0))],
        out_specs=pl.BlockSpec((bm, d), lambda i: (i, 0)),
        out_shape=jax.ShapeDtypeStruct((m, d), out_dtype),
        compiler_params=_params("parallel"),
        name="rmsnorm",
    )(x, g.reshape(1, d))


def _mm_kernel(a_ref, w_ref, o_ref):
    o_ref[...] = _dot(a_ref[...], w_ref[...]).astype(o_ref.dtype)


def _matmul(a, w, out_dtype, bm=1024, bn=512):
    m, k = a.shape
    n = w.shape[1]
    bm, bn = _blk(m, bm), _blk(n, bn)
    return pl.pallas_call(
        _mm_kernel,
        grid=(m // bm, n // bn),
        in_specs=[pl.BlockSpec((bm, k), lambda i, j: (i, 0)),
                  pl.BlockSpec((k, bn), lambda i, j: (0, j))],
        out_specs=pl.BlockSpec((bm, bn), lambda i, j: (i, j)),
        out_shape=jax.ShapeDtypeStruct((m, n), out_dtype),
        compiler_params=_params("parallel", "parallel"),
        name="matmul",
    )(a, w)


def _mm_swiglu_kernel(a_ref, wg_ref, wu_ref, o_ref):
    a = a_ref[...]
    gate = _dot(a, wg_ref[...])
    up = _dot(a, wu_ref[...])
    o_ref[...] = (jax.nn.silu(gate) * up).astype(o_ref.dtype)


def _matmul_swiglu(a, w, out_dtype, bm=1024, bn=512):
    m, k = a.shape
    f = w.shape[1] // 2
    bm, bn = _blk(m, bm), _blk(f, bn)
    nj = f // bn
    return pl.pallas_call(
        _mm_swiglu_kernel,
        grid=(m // bm, nj),
        in_specs=[pl.BlockSpec((bm, k), lambda i, j: (i, 0)),
                  pl.BlockSpec((k, bn), lambda i, j: (0, j)),
                  pl.BlockSpec((k, bn), lambda i, j: (0, j + nj))],
        out_specs=pl.BlockSpec((bm, bn), lambda i, j: (i, j)),
        out_shape=jax.ShapeDtypeStruct((m, f), out_dtype),
        compiler_params=_params("parallel", "parallel"),
        name="matmul_swiglu",
    )(a, w, w)


def _mm_kgrid_res_kernel(a_ref, w_ref, r_ref, o_ref, *, scale):
    kk = pl.program_id(2)
    part = _dot(a_ref[...], w_ref[...])

    @pl.when(kk == 0)
    def _():
        o_ref[...] = part

    @pl.when(kk > 0)
    def _():
        o_ref[...] += part

    @pl.when(kk == pl.num_programs(2) - 1)
    def _():
        o_ref[...] = r_ref[...] + scale * o_ref[...]


def _matmul_kgrid_residual(a, w, res, scale, bm=1024, bn=1024, bk=2816):
    m, k = a.shape
    n = w.shape[1]
    bm, bn, bk = _blk(m, bm), _blk(n, bn), _blk(k, bk)
    return pl.pallas_call(
        functools.partial(_mm_kgrid_res_kernel, scale=scale),
        grid=(m // bm, n // bn, k // bk),
        in_specs=[pl.BlockSpec((bm, bk), lambda i, j, kk: (i, kk)),
                  pl.BlockSpec((bk, bn), lambda i, j, kk: (kk, j)),
                  pl.BlockSpec((bm, bn), lambda i, j, kk: (i, j))],
        out_specs=pl.BlockSpec((bm, bn), lambda i, j, kk: (i, j)),
        out_shape=jax.ShapeDtypeStruct((m, n), F32),
        compiler_params=_params("parallel", "parallel", "arbitrary"),
        name="matmul_kgrid_residual",
    )(a, w, res)


def _mm_groups_res_kernel(a0_ref, a1_ref, a2_ref, a3_ref, w_ref, r_ref, o_ref):
    kg = a0_ref.shape[1]
    acc = r_ref[...]
    for g, a_ref in enumerate((a0_ref, a1_ref, a2_ref, a3_ref)):
        acc = acc + _dot(a_ref[...], w_ref[g * kg:(g + 1) * kg, :])
    o_ref[...] = acc


def _matmul_groups_residual(groups, w, res, bm=1024, bn=512):
    m, kg = groups[0].shape
    k, n = w.shape
    bm, bn = _blk(m, bm), _blk(n, bn)
    a_spec = pl.BlockSpec((bm, kg), lambda i, j: (i, 0))
    return pl.pallas_call(
        _mm_groups_res_kernel,
        grid=(m // bm, n // bn),
        in_specs=[a_spec, a_spec, a_spec, a_spec,
                  pl.BlockSpec((k, bn), lambda i, j: (0, j)),
                  pl.BlockSpec((bm, bn), lambda i, j: (i, j))],
        out_specs=pl.BlockSpec((bm, bn), lambda i, j: (i, j)),
        out_shape=jax.ShapeDtypeStruct((m, n), F32),
        compiler_params=_params("parallel", "parallel"),
        name="matmul_groups_residual",
    )(*groups, w, res)


def _conv_kernel(h_ref, w_ref, b_ref, g_ref, beta_ref, o_ref, ybuf, cbuf, *, ts, c):
    s = pl.program_id(1)

    @pl.when(s == 0)
    def _():
        ybuf[0:CONV_HALO, :] = jnp.zeros((CONV_HALO, c), F32)

    @pl.when(s > 0)
    def _():
        ybuf[0:CONV_HALO, :] = ybuf[ts:ts + CONV_HALO, :]

    a = h_ref[0, :, 0:c]
    gate = h_ref[0, :, c:2 * c]
    ybuf[CONV_HALO:CONV_HALO + ts, :] = a * jax.nn.sigmoid(gate)

    row0 = CONV_HALO - (CONV_K - 1)
    rows = 32
    for cb in range(c // LANES):
        lanes = slice(cb * LANES, (cb + 1) * LANES)
        wcol = [w_ref[j:j + 1, lanes] for j in range(CONV_K)]
        bias = b_ref[:, lanes]
        for rb in range(ts // rows):
            acc = jnp.broadcast_to(bias, (rows, LANES))
            for j in range(CONV_K):
                r0 = rb * rows + row0 + j
                acc = acc + wcol[j] * ybuf[r0:r0 + rows, lanes]
            cbuf[rb * rows:(rb + 1) * rows, lanes] = acc

    y = cbuf[...]
    mu = jnp.mean(y, axis=-1, keepdims=True)
    d = y - mu
    var = jnp.mean(d * d, axis=-1, keepdims=True)
    y = d * lax.rsqrt(var + EPS) * g_ref[...] + beta_ref[...]
    o_ref[0] = jax.nn.silu(y).astype(o_ref.dtype)


def _conv_mixer(h, w_dw, b_dw, ln_g, ln_b, ts=128):
    b, s, c2 = h.shape
    c = c2 // 2
    vec = pl.BlockSpec((1, c), lambda bi, si: (0, 0))
    return pl.pallas_call(
        functools.partial(_conv_kernel, ts=ts, c=c),
        grid=(b, s // ts),
        in_specs=[pl.BlockSpec((1, ts, c2), lambda bi, si: (bi, si, 0)),
                  pl.BlockSpec((CONV_K, c), lambda bi, si: (0, 0)),
                  vec, vec, vec],
        out_specs=pl.BlockSpec((1, ts, c), lambda bi, si: (bi, si, 0)),
        out_shape=jax.ShapeDtypeStruct((b, s, c), BF16),
        scratch_shapes=[pltpu.VMEM((ts + CONV_HALO, c), F32), pltpu.VMEM((ts, c), F32)],
        compiler_params=_params("parallel", "arbitrary"),
        name="conv_mixer",
    )(h, w_dw, b_dw.reshape(1, c), ln_g.reshape(1, c), ln_b.reshape(1, c))


def _sgu_kernel(h_ref, g_ref, beta_ref, ws_ref, bs_ref, o_ref, *, ts, c):
    hh = jax.nn.gelu(h_ref[0])
    u = hh[:, 0:c]
    v = hh[:, c:2 * c]
    mu = jnp.mean(v, axis=-1, keepdims=True)
    d = v - mu
    var = jnp.mean(d * d, axis=-1, keepdims=True)
    v = (d * lax.rsqrt(var + EPS) * g_ref[...] + beta_ref[...]).astype(BF16)
    row = lax.broadcasted_iota(jnp.int32, (SGU_BLOCK, SGU_BLOCK), 0)
    col = lax.broadcasted_iota(jnp.int32, (SGU_BLOCK, SGU_BLOCK), 1)
    chunk_causal = (col // CHUNK) <= (row // CHUNK)
    for hd in range(c // SGU_HEAD):
        lanes = slice(hd * SGU_HEAD, (hd + 1) * SGU_HEAD)
        w = jnp.where(chunk_causal, ws_ref[hd], 0.0).astype(BF16)
        for blk in range(ts // SGU_BLOCK):
            rows = slice(blk * SGU_BLOCK, (blk + 1) * SGU_BLOCK)
            sv = _dot(w, v[rows, lanes]) + bs_ref[:, lanes]
            o_ref[0, rows, lanes] = (u[rows, lanes] * sv).astype(o_ref.dtype)


def _sgu_mixer(h, ln_g, ln_b, w_s, b_s, ts=256):
    b, s, c2 = h.shape
    c = c2 // 2
    heads = w_s.shape[0]
    bias = jnp.repeat(b_s.T, c // heads, axis=1)
    vec = pl.BlockSpec((1, c), lambda bi, si: (0, 0))
    return pl.pallas_call(
        functools.partial(_sgu_kernel, ts=ts, c=c),
        grid=(b, s // ts),
        in_specs=[pl.BlockSpec((1, ts, c2), lambda bi, si: (bi, si, 0)),
                  vec, vec,
                  pl.BlockSpec((heads, SGU_BLOCK, SGU_BLOCK), lambda bi, si: (0, 0, 0)),
                  pl.BlockSpec((SGU_BLOCK, c), lambda bi, si: (0, 0))],
        out_specs=pl.BlockSpec((1, ts, c), lambda bi, si: (bi, si, 0)),
        out_shape=jax.ShapeDtypeStruct((b, s, c), BF16),
        compiler_params=_params("parallel", "parallel"),
        name="sgu_mixer",
    )(h, ln_g.reshape(1, c), ln_b.reshape(1, c), w_s, bias)


def _softplus(z):
    return jnp.maximum(z, 0.0) + jnp.log1p(jnp.exp(-jnp.abs(z)))


def _sb_kernel(q_ref, k_ref, v_ref, o_ref, *, s, scale):
    blk = SB_BLOCK
    row = lax.broadcasted_iota(jnp.int32, (blk, blk), 0)
    col = lax.broadcasted_iota(jnp.int32, (blk, blk), 1)
    before = col < row
    jrow = lax.broadcasted_iota(jnp.int32, (blk, 2 * blk), 0)
    scol = lax.broadcasted_iota(jnp.int32, (blk, 2 * blk), 1)
    u2 = jnp.where((jrow > scol) | (scol >= blk), 1.0, 0.0).astype(BF16)

    def suffix(lk):
        hi = lk.astype(BF16)
        lo = (lk - hi.astype(F32)).astype(BF16)
        r = _dot(hi, u2) + _dot(lo, u2)
        return r[:, 0:blk], r[:, blk:2 * blk]

    def pair(qb, j0, carry, acc, diag):
        kb = k_ref[0, pl.ds(j0, blk), :]
        vb = v_ref[0, pl.ds(j0, blk), :]
        z = lax.dot_general(qb, kb, (((1,), (1,)), ((), ())), preferred_element_type=F32) * scale
        sp = _softplus(z)
        lk = -sp
        if diag:
            lk = jnp.where(before, lk, 0.0)
        later, total = suffix(lk)
        wts = jnp.exp((z - sp) + (later + carry))
        if diag:
            wts = jnp.where(before, wts, 0.0)
        acc = acc + _dot(wts.astype(BF16), vb)
        return carry + total, acc

    def qblock(qi, _):
        q0 = pl.multiple_of(qi * blk, blk)
        qb = q_ref[0, pl.ds(q0, blk), :]
        zeros = jnp.zeros((blk, blk), F32)
        carry, acc = pair(qb, q0, zeros, zeros, True)

        def kstep(it, st):
            j0 = pl.multiple_of((qi - 1 - it) * blk, blk)
            return pair(qb, j0, st[0], st[1], False)

        carry, acc = lax.fori_loop(0, qi, kstep, (carry, acc))
        o_ref[0, pl.ds(q0, blk), :] = acc.astype(o_ref.dtype)
        return 0

    lax.fori_loop(0, s // blk, qblock, 0)


def _sb_mixer(h):
    b, s, c3 = h.shape
    c = c3 // 3
    heads = c // SB_HEAD
    return pl.pallas_call(
        functools.partial(_sb_kernel, s=s, scale=SB_HEAD ** -0.5),
        grid=(b, heads),
        in_specs=[pl.BlockSpec((1, s, SB_HEAD), lambda bi, hi: (bi, 0, hi)),
                  pl.BlockSpec((1, s, SB_HEAD), lambda bi, hi: (bi, 0, hi + heads)),
                  pl.BlockSpec((1, s, SB_HEAD), lambda bi, hi: (bi, 0, hi + 2 * heads))],
        out_specs=pl.BlockSpec((1, s, SB_HEAD), lambda bi, hi: (bi, 0, hi)),
        out_shape=jax.ShapeDtypeStruct((b, s, c), BF16),
        compiler_params=_params("parallel", "parallel"),
        name="sb_mixer",
    )(h, h, h)


def _rwkv_prep_kernel(h_ref, l_ref, mu_ref, mul_ref, w0_ref, wup_ref, a0_ref, aup_ref, gup_ref,
                      kk_ref, ka_ref, rk_ref,
                      v_ref, g_ref, rkr_ref, wT_ref, kkT_ref, bT_ref, kfT_ref, rT_ref,
                      hbuf, lbuf, *, ts, c):
    s = pl.program_id(1)
    c3 = 3 * c
    nl = l_ref.shape[2]

    @pl.when(s == 0)
    def _():
        hbuf[0:SUBLANES, :] = jnp.zeros((SUBLANES, c3), F32)
        lbuf[0:SUBLANES, :] = jnp.zeros((SUBLANES, nl), F32)

    @pl.when(s > 0)
    def _():
        hbuf[0:SUBLANES, :] = hbuf[ts:ts + SUBLANES, :]
        lbuf[0:SUBLANES, :] = lbuf[ts:ts + SUBLANES, :]

    h = h_ref[0]
    lo = l_ref[0]
    hbuf[SUBLANES:SUBLANES + ts, :] = h
    lbuf[SUBLANES:SUBLANES + ts, :] = lo
    h = h + (hbuf[SUBLANES - 1:SUBLANES - 1 + ts, :] - h) * mu_ref[...]
    lo = lo + (lbuf[SUBLANES - 1:SUBLANES - 1 + ts, :] - lo) * mul_ref[...]

    r = h[:, 0:c]
    k = h[:, c:2 * c]
    v = h[:, 2 * c:3 * c]
    xw = lo[:, 0:LANES]
    xa = lo[:, LANES:2 * LANES]
    xg = lo[:, 2 * LANES:nl]

    w_log = -_softplus(-(w0_ref[...] + _dot(jnp.tanh(xw).astype(BF16), wup_ref[...]))) - 0.5
    decay = jnp.exp(-jnp.exp(w_log))
    a = jax.nn.sigmoid(a0_ref[...] + _dot(xa.astype(BF16), aup_ref[...]))
    g = _dot(jax.nn.sigmoid(xg).astype(BF16), gup_ref[...])
    kk = k * kk_ref[...]
    kf = k * (1.0 + (a - 1.0) * ka_ref[...])

    v_ref[0] = v
    g_ref[0] = g
    rkr_ref[0] = r * kf * rk_ref[...]

    kkT = kk.T
    sq = kkT * kkT
    nh = c // RWKV_HEAD
    inv = []
    for hd in range(nh):
        n2 = jnp.sum(sq[hd * RWKV_HEAD:(hd + 1) * RWKV_HEAD, :], axis=0, keepdims=True)
        inv.append(jnp.broadcast_to(1.0 / jnp.maximum(jnp.sqrt(n2), 1e-12), (RWKV_HEAD, ts)))
    inv = jnp.concatenate(inv, axis=0)
    wT_ref[0] = decay.T
    kkT_ref[0] = kkT * inv
    bT_ref[0] = (kk * a).T * inv
    kfT_ref[0] = kf.T
    rT_ref[0] = r.T


def _rwkv_prep(h, lora, mu, mul, w0, w_up, a0, a_up, g_up, k_k, k_a, r_k, ts=256):
    b, s, c3 = h.shape
    c = c3 // 3
    nl = lora.shape[2]
    nat = pl.BlockSpec((1, ts, c), lambda bi, si: (bi, si, 0))
    tr = pl.BlockSpec((1, c, ts), lambda bi, si: (bi, 0, si))
    vec = lambda n: pl.BlockSpec((1, n), lambda bi, si: (0, 0))
    full = lambda a: pl.BlockSpec(a.shape, lambda bi, si: (0, 0))
    nat_shape = jax.ShapeDtypeStruct((b, s, c), F32)
    tr_shape = jax.ShapeDtypeStruct((b, c, s), F32)
    return pl.pallas_call(
        functools.partial(_rwkv_prep_kernel, ts=ts, c=c),
        grid=(b, s // ts),
        in_specs=[pl.BlockSpec((1, ts, c3), lambda bi, si: (bi, si, 0)),
                  pl.BlockSpec((1, ts, nl), lambda bi, si: (bi, si, 0)),
                  vec(c3), vec(nl), vec(c), full(w_up), vec(c), full(a_up), full(g_up),
                  vec(c), vec(c), vec(c)],
        out_specs=[nat, nat, nat, tr, tr, tr, tr, tr],
        out_shape=[nat_shape] * 3 + [tr_shape] * 5,
        scratch_shapes=[pltpu.VMEM((ts + SUBLANES, c3), F32), pltpu.VMEM((ts + SUBLANES, nl), F32)],
        compiler_params=_params("parallel", "arbitrary"),
        name="rwkv_prep",
    )(h, lora, mu.reshape(1, c3), mul.reshape(1, nl), w0.reshape(1, c), w_up, a0.reshape(1, c), a_up, g_up,
      k_k.reshape(1, c), k_a.reshape(1, c), r_k.reshape(1, c))


def _rwkv_scan_kernel(wT_ref, kkT_ref, bT_ref, kfT_ref, rT_ref, v_ref, g_ref, rkr_ref, lg_ref, lb_ref,
                      o_ref, state, ybuf, *, tt, pairs):
    n = RWKV_HEAD
    s = pl.program_id(2)

    @pl.when(s == 0)
    def _():
        state[...] = jnp.zeros_like(state)

    lane = lax.broadcasted_iota(jnp.int32, (n, LANES), 1)
    lane_full = lax.broadcasted_iota(jnp.int32, (LANES, LANES), 1)
    first = lane < n

    for p in range(pairs):
        rows = slice(p * LANES, (p + 1) * LANES)
        for half in range(tt // LANES):
            t0 = half * LANES
            tl = slice(t0, t0 + LANES)
            def column(ref, t, rows=rows, tl=tl):
                col = jnp.sum(jnp.where(lane_full == t, ref[0, rows, tl], 0.0), axis=1, keepdims=True)
                col = jnp.broadcast_to(col, (LANES, LANES))
                return jnp.where(first, col[0:n, :], col[n:2 * n, :])

            def steps(tg, st, rows=rows, t0=t0, column=column):
                r0 = pl.multiple_of(t0 + tg * SUBLANES, SUBLANES)
                v_g = v_ref[0, pl.ds(r0, SUBLANES), rows]
                ys = []
                for i in range(SUBLANES):
                    t = tg * SUBLANES + i
                    w_t, kk_t, b_t, kf_t, r_t = [column(ref, t)
                                                 for ref in (wT_ref, kkT_ref, bT_ref, kfT_ref, rT_ref)]
                    sa = -jnp.sum(st * kk_t, axis=0, keepdims=True)
                    st = st * w_t + b_t * sa + kf_t * v_g[i:i + 1, :]
                    ys.append(jnp.sum(st * r_t, axis=0, keepdims=True))
                ybuf[pl.ds(r0, SUBLANES), rows] = jnp.concatenate(ys, axis=0)
                return st

            state[p] = lax.fori_loop(0, LANES // SUBLANES, steps, state[p])

    lane_t = lax.broadcasted_iota(jnp.int32, (tt, LANES), 1)
    first_t = lane_t < n

    def head_sum(x):
        s0 = jnp.sum(jnp.where(first_t, x, 0.0), axis=1, keepdims=True)
        s1 = jnp.sum(jnp.where(first_t, 0.0, x), axis=1, keepdims=True)
        return jnp.where(first_t, s0, s1)

    for p in range(pairs):
        rows = slice(p * LANES, (p + 1) * LANES)
        y = ybuf[:, rows]
        mu = head_sum(y) * (1.0 / n)
        d = y - mu
        var = head_sum(d * d) * (1.0 / n)
        y = d * lax.rsqrt(var + LNX_EPS) * lg_ref[:, rows] + lb_ref[:, rows]
        bonus = head_sum(rkr_ref[0, :, rows]) * v_ref[0, :, rows]
        o_ref[0, :, rows] = ((y + bonus) * g_ref[0, :, rows]).astype(o_ref.dtype)


def _rwkv_scan(wT, kkT, bT, kfT, rT, v, g, rkr, lnx_g, lnx_b, tt=256, pairs=2):
    b, s, c = v.shape
    cw = pairs * LANES
    tr = pl.BlockSpec((1, cw, tt), lambda bi, pi, si: (bi, pi, si))
    nat = pl.BlockSpec((1, tt, cw), lambda bi, pi, si: (bi, si, pi))
    vec = pl.BlockSpec((1, cw), lambda bi, pi, si: (0, pi))
    return pl.pallas_call(
        functools.partial(_rwkv_scan_kernel, tt=tt, pairs=pairs),
        grid=(b, c // cw, s // tt),
        in_specs=[tr, tr, tr, tr, tr, nat, nat, nat, vec, vec],
        out_specs=nat,
        out_shape=jax.ShapeDtypeStruct((b, s, c), BF16),
        scratch_shapes=[pltpu.VMEM((pairs, RWKV_HEAD, LANES), F32), pltpu.VMEM((tt, cw), F32)],
        compiler_params=_params("parallel", "parallel", "arbitrary"),
        name="rwkv_scan",
    )(wT, kkT, bT, kfT, rT, v, g, rkr, lnx_g.reshape(1, c), lnx_b.reshape(1, c))


def _pad_rows(w, rows):
    return jnp.pad(w, ((0, rows - w.shape[0]), (0, 0)))


def _rwkv_mixer(h, lora, mu, w0, w_up, a0, a_up, g_up, k_k, k_a, r_k, lnx_g, lnx_b):
    c = h.shape[2] // 3
    mu_main = mu[:3 * c]
    mu_l = _lora_slots(mu[None, 3 * c:])[0]
    outs = _rwkv_prep(h, lora, mu_main, mu_l, w0,
                      _pad_rows(w_up, LANES).astype(BF16), a0, _pad_rows(a_up, LANES).astype(BF16),
                      _pad_rows(g_up, 2 * LANES).astype(BF16), k_k, k_a, r_k.reshape(-1))
    v, g, rkr, wT, kkT, bT, kfT, rT = outs
    return _rwkv_scan(wT, kkT, bT, kfT, rT, v, g, rkr, lnx_g, lnx_b)


def _lora_slots(w):
    xw = w[:, :DECAY_LORA]
    xa = w[:, DECAY_LORA:DECAY_LORA + ICLR_LORA]
    xg = w[:, DECAY_LORA + ICLR_LORA:]
    padto = lambda t, n: jnp.pad(t, ((0, 0), (0, n - t.shape[1])))
    return jnp.concatenate([padto(xw, LANES), padto(xa, LANES), padto(xg, 2 * LANES)], axis=1)


def _ffn(x2d, norm_g, w_in, w_out, d_ff_pad):
    d_ff = w_out.shape[0]
    gate, up = w_in[:, :d_ff], w_in[:, d_ff:]
    padc = lambda t: jnp.pad(t.astype(BF16), ((0, 0), (0, d_ff_pad - d_ff)))
    w_in_p = jnp.concatenate([padc(gate), padc(up)], axis=1)
    w_out_p = jnp.pad(w_out.astype(BF16), ((0, d_ff_pad - d_ff), (0, 0)))
    xn = _rmsnorm(x2d, norm_g, BF16)
    hid = _matmul_swiglu(xn, w_in_p, BF16)
    return _matmul_kgrid_residual(hid, w_out_p, x2d, 0.5)


def kernel(x, ffn1_norm, ffn1_w_in, ffn1_w_out, mix_norm, mix_w_in, conv_w, conv_b, conv_ln_g, conv_ln_b, rwkv_mu, rwkv_w0, rwkv_w_up, rwkv_a0, rwkv_a_up, rwkv_g_up, rwkv_k_k, rwkv_k_a, rwkv_r_k, rwkv_lnx_g, rwkv_lnx_b, sgu_ln_g, sgu_ln_b, sgu_w_s, sgu_b_s, mix_w_out, ffn2_norm, ffn2_w_in, ffn2_w_out, final_norm):
    b, s, d = x.shape
    depth = ffn1_norm.shape[0]
    c = conv_w.shape[2]
    d_ff = ffn1_w_out.shape[1]
    d_ff_pad = -(-d_ff // 1024) * 1024
    n_conv, n_rkv, n_sgu = 2 * c, 3 * c, 2 * c
    n_rwkv = n_rkv + DECAY_LORA + ICLR_LORA + GATE_LORA
    o_rwkv, o_sgu, o_sb = n_conv, n_conv + n_rwkv, n_conv + n_rwkv + n_sgu

    x2d = x.reshape(b * s, d)
    for l in range(depth):
        x2d = _ffn(x2d, ffn1_norm[l], ffn1_w_in[l], ffn1_w_out[l], d_ff_pad)

        w = mix_w_in[l]
        xn = _rmsnorm(x2d, mix_norm[l], BF16)
        mm = lambda cols, dt: _matmul(xn, cols.astype(BF16), dt).reshape(b, s, -1)
        h_conv = mm(w[:, :n_conv], F32)
        h_rkv = mm(w[:, o_rwkv:o_rwkv + n_rkv], F32)
        h_lora = mm(_lora_slots(w[:, o_rwkv + n_rkv:o_sgu]), F32)
        h_sgu = mm(w[:, o_sgu:o_sb], F32)
        h_sb = mm(w[:, o_sb:], BF16)

        y_conv = _conv_mixer(h_conv, conv_w[l], conv_b[l], conv_ln_g[l], conv_ln_b[l])
        y_rwkv = _rwkv_mixer(h_rkv, h_lora, rwkv_mu[l], rwkv_w0[l], rwkv_w_up[l], rwkv_a0[l], rwkv_a_up[l],
                             rwkv_g_up[l], rwkv_k_k[l], rwkv_k_a[l], rwkv_r_k[l], rwkv_lnx_g[l], rwkv_lnx_b[l])
        y_sgu = _sgu_mixer(h_sgu, sgu_ln_g[l], sgu_ln_b[l], sgu_w_s[l], sgu_b_s[l])
        y_sb = _sb_mixer(h_sb)
        groups = [t.reshape(b * s, c) for t in (y_conv, y_rwkv, y_sgu, y_sb)]
        x2d = _matmul_groups_residual(groups, mix_w_out[l].astype(BF16), x2d)

        x2d = _ffn(x2d, ffn2_norm[l], ffn2_w_in[l], ffn2_w_out[l], d_ff_pad)
    return _rmsnorm(x2d, final_norm, F32).reshape(b, s, d)
```

```python
import functools

import jax
import jax.numpy as jnp
from jax import lax
from jax.experimental import pallas as pl
from jax.experimental.pallas import tpu as pltpu

LANES = 128
SUBLANES = 8
VMEM_LIMIT_BYTES = 52 * 1024 * 1024

CHUNK = 64
CONV_K = 31
CONV_HALO = 32
RWKV_HEAD = 64
RWKV_CHUNK = 64
DECAY_LORA = 64
ICLR_LORA = 64
GATE_LORA = 160
LNX_EPS = 64e-5
SGU_BLOCK = 128
SGU_HEAD = 128
SB_HEAD = 128
SB_BLOCK = 128
EPS = 1e-6

BF16 = jnp.bfloat16
F32 = jnp.float32


def _params(*sem):
    return pltpu.CompilerParams(dimension_semantics=sem, vmem_limit_bytes=VMEM_LIMIT_BYTES)


_NN = (((1,), (0,)), ((), ()))
_NT = (((1,), (1,)), ((), ()))


def _dot(a, b):
    return jnp.dot(a, b, preferred_element_type=F32)


def _blk(n, pref):
    if n <= pref:
        return n
    for cand in range(pref - pref % LANES, 0, -LANES):
        if n % cand == 0:
            return cand
    return n


def _rmsnorm_kernel(x_ref, g_ref, o_ref):
    x = x_ref[...]
    ms = jnp.mean(x * x, axis=-1, keepdims=True)
    o_ref[...] = (x * lax.rsqrt(ms + EPS) * g_ref[...]).astype(o_ref.dtype)


def _rmsnorm(x, g, out_dtype, bm=256):
    m, d = x.shape
    bm = _blk(m, bm)
    return pl.pallas_call(
        _rmsnorm_kernel,
        grid=(m // bm,),
        in_specs=[pl.BlockSpec((bm, d), lambda i: (i, 0)),
                  pl.BlockSpec((1, d), lambda i: (0, 0))],
        out_specs=pl.BlockSpec((bm, d), lambda i: (i, 0)),
        out_shape=jax.ShapeDtypeStruct((m, d), out_dtype),
        compiler_params=_params("parallel"),
        name="rmsnorm",
    )(x, g.reshape(1, d))


def _mm_kernel(a_ref, w_ref, o_ref):
    o_ref[...] = _dot(a_ref[...], w_ref[...]).astype(o_ref.dtype)


def _matmul(a, w, out_dtype, bm=1024, bn=512):
    m, k = a.shape
    n = w.shape[1]
    bm, bn = _blk(m, bm), _blk(n, bn)
    return pl.pallas_call(
        _mm_kernel,
        grid=(m // bm, n // bn),
        in_specs=[pl.BlockSpec((bm, k), lambda i, j: (i, 0)),
                  pl.BlockSpec((k, bn), lambda i, j: (0, j))],
        out_specs=pl.BlockSpec((bm, bn), lambda i, j: (i, j)),
        out_shape=jax.ShapeDtypeStruct((m, n), out_dtype),
        compiler_params=_params("parallel", "parallel"),
        name="matmul",
    )(a, w)


def _mm_swiglu_kernel(a_ref, wg_ref, wu_ref, o_ref, *, nj):
    @pl.when(pl.program_id(1) < nj)
    def _():
        a = a_ref[...]
        gate = _dot(a, wg_ref[...].astype(BF16))
        up = _dot(a, wu_ref[...].astype(BF16))
        o_ref[...] = (jax.nn.silu(gate) * up).astype(o_ref.dtype)

    @pl.when(pl.program_id(1) >= nj)
    def _():
        o_ref[...] = jnp.zeros_like(o_ref)


def _matmul_swiglu(a, w, f_pad, out_dtype, bm=2048, bn=256):
    m, k = a.shape
    f = w.shape[1] // 2
    bm, bn = _blk(m, bm), _blk(f, bn)
    nj = f // bn
    col = lambda j: jnp.minimum(j, nj - 1)
    return pl.pallas_call(
        functools.partial(_mm_swiglu_kernel, nj=nj),
        grid=(m // bm, f_pad // bn),
        in_specs=[pl.BlockSpec((bm, k), lambda i, j: (i, 0), pipeline_mode=pl.Buffered(1)),
                  pl.BlockSpec((k, bn), lambda i, j: (0, col(j))),
                  pl.BlockSpec((k, bn), lambda i, j: (0, col(j) + nj))],
        out_specs=pl.BlockSpec((bm, bn), lambda i, j: (i, j)),
        out_shape=jax.ShapeDtypeStruct((m, f_pad), out_dtype),
        compiler_params=_params("parallel", "arbitrary"),
        name="matmul_swiglu",
    )(a, w, w)


def _mm_kgrid_res_kernel(a_ref, w_ref, r_ref, o_ref, *, scale):
    kk = pl.program_id(2)
    part = _dot(a_ref[...], w_ref[...])

    @pl.when(kk == 0)
    def _():
        o_ref[...] = part

    @pl.when(kk > 0)
    def _():
        o_ref[...] += part

    @pl.when(kk == pl.num_programs(2) - 1)
    def _():
        o_ref[...] = r_ref[...] + scale * o_ref[...]


def _matmul_kgrid_residual(a, w, res, scale, bm=1024, bn=1024, bk=2816):
    m, k = a.shape
    n = w.shape[1]
    bm, bn, bk = _blk(m, bm), _blk(n, bn), _blk(k, bk)
    return pl.pallas_call(
        functools.partial(_mm_kgrid_res_kernel, scale=scale),
        grid=(m // bm, n // bn, k // bk),
        in_specs=[pl.BlockSpec((bm, bk), lambda i, j, kk: (i, kk)),
                  pl.BlockSpec((bk, bn), lambda i, j, kk: (kk, j)),
                  pl.BlockSpec((bm, bn), lambda i, j, kk: (i, j))],
        out_specs=pl.BlockSpec((bm, bn), lambda i, j, kk: (i, j)),
        out_shape=jax.ShapeDtypeStruct((m, n), F32),
        compiler_params=_params("parallel", "parallel", "arbitrary"),
        name="matmul_kgrid_residual",
    )(a, w, res)


def _mm_groups_res_kernel(a0_ref, a1_ref, a2_ref, a3_ref, w_ref, r_ref, o_ref):
    kg = a0_ref.shape[1]
    acc = r_ref[...]
    for g, a_ref in enumerate((a0_ref, a1_ref, a2_ref, a3_ref)):
        acc = acc + _dot(a_ref[...], w_ref[g * kg:(g + 1) * kg, :])
    o_ref[...] = acc


def _matmul_groups_residual(groups, w, res, bm=1024, bn=512):
    m, kg = groups[0].shape
    k, n = w.shape
    bm, bn = _blk(m, bm), _blk(n, bn)
    a_spec = pl.BlockSpec((bm, kg), lambda i, j: (i, 0))
    return pl.pallas_call(
        _mm_groups_res_kernel,
        grid=(m // bm, n // bn),
        in_specs=[a_spec, a_spec, a_spec, a_spec,
                  pl.BlockSpec((k, bn), lambda i, j: (0, j)),
                  pl.BlockSpec((bm, bn), lambda i, j: (i, j))],
        out_specs=pl.BlockSpec((bm, bn), lambda i, j: (i, j)),
        out_shape=jax.ShapeDtypeStruct((m, n), F32),
        compiler_params=_params("parallel", "parallel"),
        name="matmul_groups_residual",
    )(*groups, w, res)


def _conv_kernel(h_ref, w_ref, b_ref, g_ref, beta_ref, o_ref, ybuf, cbuf, *, ts, c):
    s = pl.program_id(1)

    @pl.when(s == 0)
    def _():
        ybuf[0:CONV_HALO, :] = jnp.zeros((CONV_HALO, c), F32)

    @pl.when(s > 0)
    def _():
        ybuf[0:CONV_HALO, :] = ybuf[ts:ts + CONV_HALO, :]

    a = h_ref[0, :, 0:c]
    gate = h_ref[0, :, c:2 * c]
    ybuf[CONV_HALO:CONV_HALO + ts, :] = a * jax.nn.sigmoid(gate)

    row0 = CONV_HALO - (CONV_K - 1)
    rows = 32
    for cb in range(c // LANES):
        lanes = slice(cb * LANES, (cb + 1) * LANES)
        wcol = [w_ref[j:j + 1, lanes] for j in range(CONV_K)]
        bias = b_ref[:, lanes]
        for rb in range(ts // rows):
            acc = jnp.broadcast_to(bias, (rows, LANES))
            for j in range(CONV_K):
                r0 = rb * rows + row0 + j
                acc = acc + wcol[j] * ybuf[r0:r0 + rows, lanes]
            cbuf[rb * rows:(rb + 1) * rows, lanes] = acc

    y = cbuf[...]
    mu = jnp.mean(y, axis=-1, keepdims=True)
    d = y - mu
    var = jnp.mean(d * d, axis=-1, keepdims=True)
    y = d * lax.rsqrt(var + EPS) * g_ref[...] + beta_ref[...]
    o_ref[0] = jax.nn.silu(y).astype(o_ref.dtype)


def _conv_mixer(h, w_dw, b_dw, ln_g, ln_b, ts=128):
    b, s, c2 = h.shape
    c = c2 // 2
    vec = pl.BlockSpec((1, c), lambda bi, si: (0, 0))
    return pl.pallas_call(
        functools.partial(_conv_kernel, ts=ts, c=c),
        grid=(b, s // ts),
        in_specs=[pl.BlockSpec((1, ts, c2), lambda bi, si: (bi, si, 0)),
                  pl.BlockSpec((CONV_K, c), lambda bi, si: (0, 0)),
                  vec, vec, vec],
        out_specs=pl.BlockSpec((1, ts, c), lambda bi, si: (bi, si, 0)),
        out_shape=jax.ShapeDtypeStruct((b, s, c), BF16),
        scratch_shapes=[pltpu.VMEM((ts + CONV_HALO, c), F32), pltpu.VMEM((ts, c), F32)],
        compiler_params=_params("parallel", "arbitrary"),
        name="conv_mixer",
    )(h, w_dw, b_dw.reshape(1, c), ln_g.reshape(1, c), ln_b.reshape(1, c))


def _sgu_kernel(h_ref, g_ref, beta_ref, ws_ref, bs_ref, o_ref, *, ts, c):
    hh = jax.nn.gelu(h_ref[0])
    u = hh[:, 0:c]
    v = hh[:, c:2 * c]
    mu = jnp.mean(v, axis=-1, keepdims=True)
    d = v - mu
    var = jnp.mean(d * d, axis=-1, keepdims=True)
    v = (d * lax.rsqrt(var + EPS) * g_ref[...] + beta_ref[...]).astype(BF16)
    row = lax.broadcasted_iota(jnp.int32, (SGU_BLOCK, SGU_BLOCK), 0)
    col = lax.broadcasted_iota(jnp.int32, (SGU_BLOCK, SGU_BLOCK), 1)
    chunk_causal = (col // CHUNK) <= (row // CHUNK)
    for hd in range(c // SGU_HEAD):
        lanes = slice(hd * SGU_HEAD, (hd + 1) * SGU_HEAD)
        w = jnp.where(chunk_causal, ws_ref[hd], 0.0).astype(BF16)
        for blk in range(ts // SGU_BLOCK):
            rows = slice(blk * SGU_BLOCK, (blk + 1) * SGU_BLOCK)
            sv = _dot(w, v[rows, lanes]) + bs_ref[:, lanes]
            o_ref[0, rows, lanes] = (u[rows, lanes] * sv).astype(o_ref.dtype)


def _sgu_mixer(h, ln_g, ln_b, w_s, b_s, ts=256):
    b, s, c2 = h.shape
    c = c2 // 2
    heads = w_s.shape[0]
    bias = jnp.repeat(b_s.T, c // heads, axis=1)
    vec = pl.BlockSpec((1, c), lambda bi, si: (0, 0))
    return pl.pallas_call(
        functools.partial(_sgu_kernel, ts=ts, c=c),
        grid=(b, s // ts),
        in_specs=[pl.BlockSpec((1, ts, c2), lambda bi, si: (bi, si, 0)),
                  vec, vec,
                  pl.BlockSpec((heads, SGU_BLOCK, SGU_BLOCK), lambda bi, si: (0, 0, 0)),
                  pl.BlockSpec((SGU_BLOCK, c), lambda bi, si: (0, 0))],
        out_specs=pl.BlockSpec((1, ts, c), lambda bi, si: (bi, si, 0)),
        out_shape=jax.ShapeDtypeStruct((b, s, c), BF16),
        compiler_params=_params("parallel", "parallel"),
        name="sgu_mixer",
    )(h, ln_g.reshape(1, c), ln_b.reshape(1, c), w_s, bias)


def _softplus(z):
    return jnp.maximum(z, 0.0) + jnp.log1p(jnp.exp(-jnp.abs(z)))


def _sb_kernel(q_ref, k_ref, v_ref, o_ref, *, s, scale, heads):
    blk = SB_BLOCK
    row = lax.broadcasted_iota(jnp.int32, (blk, blk), 0)
    col = lax.broadcasted_iota(jnp.int32, (blk, blk), 1)
    before = col < row
    jrow = lax.broadcasted_iota(jnp.int32, (blk, 2 * blk), 0)
    scol = lax.broadcasted_iota(jnp.int32, (blk, 2 * blk), 1)
    u2 = jnp.where((jrow > scol) | (scol >= blk), 1.0, 0.0).astype(BF16)

    def suffix(lk):
        hi = lk.astype(BF16)
        lo = (lk - hi.astype(F32)).astype(BF16)
        r = _dot(hi, u2) + _dot(lo, u2)
        return r[:, 0:blk], r[:, blk:2 * blk]

    head_lanes = [slice(hd * SB_HEAD, (hd + 1) * SB_HEAD) for hd in range(heads)]

    def qblock(qi, _):
        q0 = pl.multiple_of(qi * blk, blk)
        qbs = [q_ref[0, pl.ds(q0, blk), lanes] for lanes in head_lanes]

        def key_block(j0, st, diag):
            z = [lax.dot_general(qb, k_ref[0, pl.ds(j0, blk), lanes], _NT, preferred_element_type=F32) * scale
                 for qb, lanes in zip(qbs, head_lanes)]
            sp = [_softplus(x) for x in z]
            lk = [-x for x in sp]
            if diag:
                lk = [jnp.where(before, x, 0.0) for x in lk]
            suf = [suffix(x) for x in lk]
            wts = [jnp.exp((zz - ss) + (later + carry))
                   for zz, ss, (later, _), (carry, _) in zip(z, sp, suf, st)]
            if diag:
                wts = [jnp.where(before, x, 0.0) for x in wts]
            pv = [_dot(x.astype(BF16), v_ref[0, pl.ds(j0, blk), lanes]) for x, lanes in zip(wts, head_lanes)]
            return tuple((carry + total, acc + p) for (carry, acc), (_, total), p in zip(st, suf, pv))

        zeros = jnp.zeros((blk, blk), F32)
        st = key_block(q0, ((zeros, zeros),) * heads, True)

        def kstep(it, st):
            return key_block(pl.multiple_of((qi - 1 - it) * blk, blk), st, False)

        st = lax.fori_loop(0, qi, kstep, st)
        for hd, lanes in enumerate(head_lanes):
            o_ref[0, pl.ds(q0, blk), lanes] = st[hd][1].astype(o_ref.dtype)
        return 0

    lax.fori_loop(0, s // blk, qblock, 0)


def _sb_mixer(h, heads_per_step=8):
    b, s, c3 = h.shape
    c = c3 // 3
    w = heads_per_step * SB_HEAD
    groups = c // w
    return pl.pallas_call(
        functools.partial(_sb_kernel, s=s, scale=SB_HEAD ** -0.5, heads=heads_per_step),
        grid=(b, groups),
        in_specs=[pl.BlockSpec((1, s, w), lambda bi, gi: (bi, 0, gi)),
                  pl.BlockSpec((1, s, w), lambda bi, gi: (bi, 0, gi + groups)),
                  pl.BlockSpec((1, s, w), lambda bi, gi: (bi, 0, gi + 2 * groups))],
        out_specs=pl.BlockSpec((1, s, w), lambda bi, gi: (bi, 0, gi)),
        out_shape=jax.ShapeDtypeStruct((b, s, c), BF16),
        compiler_params=_params("parallel", "parallel"),
        name="sb_mixer",
    )(h, h, h)


def _rwkv_prep_kernel(h_ref, l_ref, mu_ref, mul_ref, w0_ref, wup_ref, a0_ref, aup_ref, gup_ref, kk_ref, ka_ref,
                      r_ref, lw_ref, kkr_ref, a_ref, kf_ref, v_ref, g_ref, hbuf, lbuf, *, ts, c):
    s = pl.program_id(1)
    c3 = 3 * c
    nl = l_ref.shape[2]

    @pl.when(s == 0)
    def _():
        hbuf[0:SUBLANES, :] = jnp.zeros((SUBLANES, c3), F32)
        lbuf[0:SUBLANES, :] = jnp.zeros((SUBLANES, nl), F32)

    @pl.when(s > 0)
    def _():
        hbuf[0:SUBLANES, :] = hbuf[ts:ts + SUBLANES, :]
        lbuf[0:SUBLANES, :] = lbuf[ts:ts + SUBLANES, :]

    h = h_ref[0]
    lo = l_ref[0]
    hbuf[SUBLANES:SUBLANES + ts, :] = h
    lbuf[SUBLANES:SUBLANES + ts, :] = lo
    h = h + (hbuf[SUBLANES - 1:SUBLANES - 1 + ts, :] - h) * mu_ref[...]
    lo = lo + (lbuf[SUBLANES - 1:SUBLANES - 1 + ts, :] - lo) * mul_ref[...]

    r = h[:, 0:c]
    k = h[:, c:2 * c]
    xw = lo[:, 0:LANES]
    xa = lo[:, LANES:2 * LANES]
    xg = lo[:, 2 * LANES:nl]

    w_log = -_softplus(-(w0_ref[...] + _dot(jnp.tanh(xw).astype(BF16), wup_ref[...]))) - 0.5
    a = jax.nn.sigmoid(a0_ref[...] + _dot(xa.astype(BF16), aup_ref[...]))
    r_ref[0] = r
    lw_ref[0] = -jnp.exp(w_log)
    kkr_ref[0] = k * kk_ref[...]
    a_ref[0] = a
    kf_ref[0] = k * (1.0 + (a - 1.0) * ka_ref[...])
    v_ref[0] = h[:, 2 * c:3 * c]
    g_ref[0] = _dot(jax.nn.sigmoid(xg).astype(BF16), gup_ref[...])


def _rwkv_prep(h, lora, mu, mul, w0, w_up, a0, a_up, g_up, k_k, k_a, ts=256):
    b, s, c3 = h.shape
    c = c3 // 3
    nl = lora.shape[2]
    nat = pl.BlockSpec((1, ts, c), lambda bi, si: (bi, si, 0))
    vec = lambda n: pl.BlockSpec((1, n), lambda bi, si: (0, 0))
    full = lambda a: pl.BlockSpec(a.shape, lambda bi, si: (0, 0))
    return pl.pallas_call(
        functools.partial(_rwkv_prep_kernel, ts=ts, c=c),
        grid=(b, s // ts),
        in_specs=[pl.BlockSpec((1, ts, c3), lambda bi, si: (bi, si, 0)),
                  pl.BlockSpec((1, ts, nl), lambda bi, si: (bi, si, 0)),
                  vec(c3), vec(nl), vec(c), full(w_up), vec(c), full(a_up), full(g_up), vec(c), vec(c)],
        out_specs=[nat] * 7,
        out_shape=[jax.ShapeDtypeStruct((b, s, c), F32)] * 7,
        scratch_shapes=[pltpu.VMEM((ts + SUBLANES, c3), F32), pltpu.VMEM((ts + SUBLANES, nl), F32)],
        compiler_params=_params("parallel", "arbitrary"),
        name="rwkv_prep",
    )(h, lora, mu.reshape(1, c3), mul.reshape(1, nl), w0.reshape(1, c), w_up, a0.reshape(1, c), a_up, g_up,
      k_k.reshape(1, c), k_a.reshape(1, c))


def _split(x):
    hi = x.astype(BF16)
    return hi, (x - hi.astype(F32)).astype(BF16)


def _dot3(a, b, dims=_NN):
    lhs = jnp.concatenate([a[0], a[1]], axis=1)
    zero = jnp.zeros_like(b[0])
    if dims is _NN:
        rhs = jnp.concatenate([jnp.concatenate([b[0], b[1]], axis=1),
                               jnp.concatenate([b[0], zero], axis=1)], axis=0)
    else:
        rhs = jnp.concatenate([jnp.concatenate([b[0], b[0]], axis=1),
                               jnp.concatenate([b[1], zero], axis=1)], axis=0)
    out = lax.dot_general(lhs, rhs, dims, preferred_element_type=F32)
    n = out.shape[1] // 2
    return out[:, :n] + out[:, n:]


def _rwkv_chunk_kernel(r_ref, lw_ref, kkr_ref, a_ref, kf_ref, v_ref, g_ref, rk_ref, lg_ref, lb_ref,
                       o_ref, state, *, tt, groups):
    n, t = RWKV_HEAD, RWKV_CHUNK
    t2 = 2 * t

    @pl.when(pl.program_id(2) == 0)
    def _():
        state[...] = jnp.zeros_like(state)

    first = lax.broadcasted_iota(jnp.int32, (t, LANES), 1) < n
    row = lax.broadcasted_iota(jnp.int32, (t2, t2), 0)
    col = lax.broadcasted_iota(jnp.int32, (t2, t2), 1)
    strict, incl = row > col, row >= col
    eye = jnp.where(row == col, 1.0, 0.0)
    tri3 = jnp.where(lax.broadcasted_iota(jnp.int32, (t, 3 * t), 0) >= lax.broadcasted_iota(jnp.int32, (t, 3 * t), 1) % t,
                     1.0, 0.0).astype(BF16)

    def head_sum(x):
        s0 = jnp.sum(jnp.where(first, x, 0.0), axis=1, keepdims=True)
        s1 = jnp.sum(jnp.where(first, 0.0, x), axis=1, keepdims=True)
        return jnp.where(first, s0, s1)

    def stack(x):
        return jnp.concatenate([jnp.where(first, x, 0.0), jnp.where(first, 0.0, x)], axis=0)

    cat = lambda *xs: jnp.concatenate(xs, axis=0)
    nch = tt // t
    units = [(slice(c * t, (c + 1) * t), slice(g * LANES, (g + 1) * LANES)) for g in range(groups) for c in range(nch)]
    each = lambda fn, *lists: [fn(*xs) for xs in zip(*lists)]

    def setup(rw, ln):
        r, lw, kkr, a, kf, v = (ref[0, rw, ln] for ref in (r_ref, lw_ref, kkr_ref, a_ref, kf_ref, v_ref))
        h1 = lw.astype(BF16)
        r1 = lw - h1.astype(F32)
        h2 = r1.astype(BF16)
        h3 = (r1 - h2.astype(F32)).astype(BF16)
        cum = _dot(tri3, cat(h1, h2, h3))
        cum_t = cum[t - 1:t, :]
        kk = kkr / jnp.maximum(jnp.sqrt(head_sum(kkr * kkr)), 1e-12)
        b = kk * a
        inv_p = jnp.exp(-cum)
        to_end = jnp.exp(cum_t - cum)
        ks = stack(kk * jnp.exp(cum - lw))
        kr = _split(cat(ks, stack(r * jnp.exp(cum))))
        bp = _split(cat(stack(b * inv_p), stack(kf * inv_p)))
        ends = _split(cat(stack(b * to_end), stack(kf * to_end)))
        bonus = head_sum(r * kf * rk_ref[:, ln]) * v
        return ks, kr, bp, ends, stack(v), jnp.exp(cum_t), bonus

    ks, kr, bp, ends, vs, decay_t, bonus = zip(*[setup(rw, ln) for rw, ln in units])
    gram = each(lambda x, y: _dot3(x, y, _NT), kr, bp)
    a_m = [jnp.where(strict, g[:t2, :t2], 0.0) for g in gram]
    catp = lambda x, y: (cat(x[0], y[0]), cat(x[1], y[1]))
    minv = [eye - x for x in a_m]
    pw = [_split(x) for x in a_m]
    pw = [_split(_dot3(x, x)) for x in pw]
    for _ in range(t.bit_length() - 3):
        both = each(lambda p, m: _dot3(catp(p, _split(m)), p), pw, minv)
        minv = each(lambda m, bo: m + bo[t2:], minv, both)
        pw = [_split(bo[:t2]) for bo in both]
    minv = each(lambda m, p: m + _dot3(_split(m), p), minv, pw)
    ms = [_split(m) for m in minv]
    vss = [_split(x) for x in vs]
    lv = each(lambda g, x: _dot3(_split(cat(jnp.where(strict, g[:t2, t2:], 0.0),
                                            jnp.where(incl, g[t2:, t2:], 0.0))), x), gram, vss)
    ds = [x[t2:] for x in lv]
    ktcs = each(lambda m, x, y: _dot3(m, _split(jnp.concatenate([x, y[:t2]], axis=1))), ms, ks, lv)
    kt = [_split(x[:, :LANES]) for x in ktcs]
    cs = [x[:, LANES:] for x in ktcs]
    lbs = [_split(jnp.where(incl, g[t2:, :t2], 0.0)) for g in gram]
    lhs = each(lambda k, x: (cat(k[0], x[0][t2:]), cat(k[1], x[1][t2:])), kt, kr)

    st = [state[g] for g in range(groups)]
    for c in range(nch):
        ids = [g * nch + c for g in range(groups)]
        x = [_dot3(lhs[i], _split(s_g), _NT) for i, s_g in zip(ids, st)]
        us = [-xx[:t2] - cs[i] for i, xx in zip(ids, x)]
        ys = [xx[t2:] + _dot3(lbs[i], _split(u)) + ds[i] for i, xx, u in zip(ids, x, us)]
        uv_t = [cat(u, vs[i]).T for i, u in zip(ids, us)]
        st = [s_g * decay_t[i] + _dot3(_split(m), ends[i]) for i, s_g, m in zip(ids, st, uv_t)]
        for i, yy in zip(ids, ys):
            rw, ln = units[i]
            y = yy[:t] + yy[t:]
            mu = head_sum(y) * (1.0 / n)
            d = y - mu
            var = head_sum(d * d) * (1.0 / n)
            y = d * lax.rsqrt(var + LNX_EPS) * lg_ref[:, ln] + lb_ref[:, ln]
            o_ref[0, rw, ln] = ((y + bonus[i]) * g_ref[0, rw, ln]).astype(o_ref.dtype)
    for g in range(groups):
        state[g] = st[g]


def _rwkv_chunks(r, lw, kkr, a, kf, v, g, r_k, lnx_g, lnx_b, tt=128, groups=4):
    b, s, c = v.shape
    tt = _blk(s, tt)
    cw = groups * LANES
    nat = pl.BlockSpec((1, tt, cw), lambda bi, pi, si: (bi, si, pi))
    vec = pl.BlockSpec((1, cw), lambda bi, pi, si: (0, pi))
    return pl.pallas_call(
        functools.partial(_rwkv_chunk_kernel, tt=tt, groups=groups),
        grid=(b, c // cw, s // tt),
        in_specs=[nat] * 7 + [vec] * 3,
        out_specs=nat,
        out_shape=jax.ShapeDtypeStruct((b, s, c), BF16),
        scratch_shapes=[pltpu.VMEM((groups, LANES, LANES), F32)],
        compiler_params=_params("parallel", "parallel", "arbitrary"),
        name="rwkv_chunks",
    )(r, lw, kkr, a, kf, v, g, r_k.reshape(1, c), lnx_g.reshape(1, c), lnx_b.reshape(1, c))


def _pad_rows(w, rows):
    return jnp.pad(w, ((0, rows - w.shape[0]), (0, 0)))


def _rwkv_mixer(h, lora, mu, w0, w_up, a0, a_up, g_up, k_k, k_a, r_k, lnx_g, lnx_b):
    c = h.shape[2] // 3
    mu_main = mu[:3 * c]
    mu_l = _lora_slots(mu[None, 3 * c:])[0]
    r, lw, kkr, a, kf, v, g = _rwkv_prep(
        h, lora, mu_main, mu_l, w0, _pad_rows(w_up, LANES).astype(BF16), a0,
        _pad_rows(a_up, LANES).astype(BF16), _pad_rows(g_up, 2 * LANES).astype(BF16), k_k, k_a)
    return _rwkv_chunks(r, lw, kkr, a, kf, v, g, r_k.reshape(-1), lnx_g, lnx_b)


def _lora_slots(w):
    xw = w[:, :DECAY_LORA]
    xa = w[:, DECAY_LORA:DECAY_LORA + ICLR_LORA]
    xg = w[:, DECAY_LORA + ICLR_LORA:]
    padto = lambda t, n: jnp.pad(t, ((0, 0), (0, n - t.shape[1])))
    return jnp.concatenate([padto(xw, LANES), padto(xa, LANES), padto(xg, 2 * LANES)], axis=1)


def _ffn(x2d, norm_g, w_in, w_out, d_ff_pad):
    d_ff = w_out.shape[0]
    w_out_p = jnp.pad(w_out.astype(BF16), ((0, d_ff_pad - d_ff), (0, 0)))
    xn = _rmsnorm(x2d, norm_g, BF16)
    hid = _matmul_swiglu(xn, w_in, d_ff_pad, BF16)
    return _matmul_kgrid_residual(hid, w_out_p, x2d, 0.5)


def kernel(x, ffn1_norm, ffn1_w_in, ffn1_w_out, mix_norm, mix_w_in, conv_w, conv_b, conv_ln_g, conv_ln_b, rwkv_mu, rwkv_w0, rwkv_w_up, rwkv_a0, rwkv_a_up, rwkv_g_up, rwkv_k_k, rwkv_k_a, rwkv_r_k, rwkv_lnx_g, rwkv_lnx_b, sgu_ln_g, sgu_ln_b, sgu_w_s, sgu_b_s, mix_w_out, ffn2_norm, ffn2_w_in, ffn2_w_out, final_norm):
    b, s, d = x.shape
    depth = ffn1_norm.shape[0]
    c = conv_w.shape[2]
    d_ff = ffn1_w_out.shape[1]
    d_ff_pad = -(-d_ff // 1024) * 1024
    n_conv, n_rkv, n_sgu = 2 * c, 3 * c, 2 * c
    n_rwkv = n_rkv + DECAY_LORA + ICLR_LORA + GATE_LORA
    o_rwkv, o_sgu, o_sb = n_conv, n_conv + n_rwkv, n_conv + n_rwkv + n_sgu

    x2d = x.reshape(b * s, d)
    for l in range(depth):
        x2d = _ffn(x2d, ffn1_norm[l], ffn1_w_in[l], ffn1_w_out[l], d_ff_pad)

        w = mix_w_in[l]
        xn = _rmsnorm(x2d, mix_norm[l], BF16)
        mm = lambda cols, dt: _matmul(xn, cols.astype(BF16), dt).reshape(b, s, -1)
        h_conv = mm(w[:, :n_conv], F32)
        h_rkv = mm(w[:, o_rwkv:o_rwkv + n_rkv], F32)
        h_lora = mm(_lora_slots(w[:, o_rwkv + n_rkv:o_sgu]), F32)
        h_sgu = mm(w[:, o_sgu:o_sb], F32)
        h_sb = mm(w[:, o_sb:], BF16)

        y_conv = _conv_mixer(h_conv, conv_w[l], conv_b[l], conv_ln_g[l], conv_ln_b[l])
        y_rwkv = _rwkv_mixer(h_rkv, h_lora, rwkv_mu[l], rwkv_w0[l], rwkv_w_up[l], rwkv_a0[l], rwkv_a_up[l],
                             rwkv_g_up[l], rwkv_k_k[l], rwkv_k_a[l], rwkv_r_k[l], rwkv_lnx_g[l], rwkv_lnx_b[l])
        y_sgu = _sgu_mixer(h_sgu, sgu_ln_g[l], sgu_ln_b[l], sgu_w_s[l], sgu_b_s[l])
        y_sb = _sb_mixer(h_sb)
        groups = [t.reshape(b * s, c) for t in (y_conv, y_rwkv, y_sgu, y_sb)]
        x2d = _matmul_groups_residual(groups, mix_w_out[l].astype(BF16), x2d)

        x2d = _ffn(x2d, ffn2_norm[l], ffn2_w_in[l], ffn2_w_out[l], d_ff_pad)
    return _rmsnorm(x2d, final_norm, F32).reshape(b, s, d)
```

```python
import functools

import jax
import jax.numpy as jnp
from jax import lax
from jax.experimental import pallas as pl
from jax.experimental.pallas import tpu as pltpu

LANES = 128
SUBLANES = 8
VMEM_LIMIT_BYTES = 52 * 1024 * 1024

CHUNK = 64
CONV_K = 31
CONV_HALO = 32
RWKV_HEAD = 64
RWKV_CHUNK = 64
DECAY_LORA = 64
ICLR_LORA = 64
GATE_LORA = 160
LNX_EPS = 64e-5
SGU_BLOCK = 128
SGU_HEAD = 128
SB_HEAD = 128
SB_BLOCK = 128
EPS = 1e-6

BF16 = jnp.bfloat16
F32 = jnp.float32


def _params(*sem):
    return pltpu.CompilerParams(dimension_semantics=sem, vmem_limit_bytes=VMEM_LIMIT_BYTES)


_NN = (((1,), (0,)), ((), ()))
_NT = (((1,), (1,)), ((), ()))


def _dot(a, b):
    return jnp.dot(a, b, preferred_element_type=F32)


def _blk(n, pref):
    if n <= pref:
        return n
    for cand in range(pref - pref % LANES, 0, -LANES):
        if n % cand == 0:
            return cand
    return n


def _rmsnorm_kernel(x_ref, g_ref, o_ref):
    x = x_ref[...]
    ms = jnp.mean(x * x, axis=-1, keepdims=True)
    o_ref[...] = (x * lax.rsqrt(ms + EPS) * g_ref[...]).astype(o_ref.dtype)


def _rmsnorm(x, g, out_dtype, bm=256):
    m, d = x.shape
    bm = _blk(m, bm)
    return pl.pallas_call(
        _rmsnorm_kernel,
        grid=(m // bm,),
        in_specs=[pl.BlockSpec((bm, d), lambda i: (i, 0)),
                  pl.BlockSpec((1, d), lambda i: (0, 0))],
        out_specs=pl.BlockSpec((bm, d), lambda i: (i, 0)),
        out_shape=jax.ShapeDtypeStruct((m, d), out_dtype),
        compiler_params=_params("parallel"),
        name="rmsnorm",
    )(x, g.reshape(1, d))


def _mm_nt_kernel(a_ref, w_ref, o_ref):
    o_ref[...] = lax.dot_general(a_ref[...], w_ref[0].astype(BF16), _NT,
                                 preferred_element_type=F32).astype(o_ref.dtype)


def _matmul_nt(a, wt, layer, row0, n, out_dtype, bm=1024, bn=512):
    m, k = a.shape
    bm, bn = _blk(m, bm), _blk(n, bn)
    assert row0 % SUBLANES == 0
    w_spec = pl.BlockSpec((pl.Element(1), pl.Element(bn), pl.Element(k)),
                          lambda i, j: (layer, pl.multiple_of(row0 + j * bn, SUBLANES), 0))
    return pl.pallas_call(
        _mm_nt_kernel,
        grid=(m // bm, n // bn),
        in_specs=[pl.BlockSpec((bm, k), lambda i, j: (i, 0)), w_spec],
        out_specs=pl.BlockSpec((bm, bn), lambda i, j: (i, j)),
        out_shape=jax.ShapeDtypeStruct((m, n), out_dtype),
        compiler_params=_params("parallel", "parallel"),
        name="matmul_nt",
    )(a, wt)


def _mm_swiglu_kernel(a_ref, wg_ref, wu_ref, o_ref):
    a = a_ref[...]
    gate = _dot(a, wg_ref[...].astype(BF16))
    up = _dot(a, wu_ref[...].astype(BF16))
    o_ref[...] = (jax.nn.silu(gate) * up).astype(o_ref.dtype)


def _matmul_swiglu(a, w, layer, out_dtype, bm=2048, bn=256):
    m, k = a.shape
    f = w.shape[2] // 2
    bm, bn = _blk(m, bm), _blk(f, bn)
    nj = f // bn
    return pl.pallas_call(
        _mm_swiglu_kernel,
        grid=(m // bm, nj),
        in_specs=[pl.BlockSpec((bm, k), lambda i, j: (i, 0), pipeline_mode=pl.Buffered(1)),
                  pl.BlockSpec((None, k, bn), lambda i, j: (layer, 0, j)),
                  pl.BlockSpec((None, k, bn), lambda i, j: (layer, 0, j + nj))],
        out_specs=pl.BlockSpec((bm, bn), lambda i, j: (i, j)),
        out_shape=jax.ShapeDtypeStruct((m, f), out_dtype),
        compiler_params=_params("parallel", "arbitrary"),
        name="matmul_swiglu",
    )(a, w, w)


def _mm_res_kernel(a_ref, w_ref, r_ref, o_ref, *, scale):
    o_ref[...] = r_ref[...] + scale * _dot(a_ref[...], w_ref[...])


def _matmul_residual(a, w, layer, res, scale, bm=512, bn=512):
    m, k = a.shape
    n = w.shape[2]
    bm, bn = _blk(m, bm), _blk(n, bn)
    return pl.pallas_call(
        functools.partial(_mm_res_kernel, scale=scale),
        grid=(m // bm, n // bn),
        in_specs=[pl.BlockSpec((bm, k), lambda i, j: (i, 0)),
                  pl.BlockSpec((None, k, bn), lambda i, j: (layer, 0, j)),
                  pl.BlockSpec((bm, bn), lambda i, j: (i, j))],
        out_specs=pl.BlockSpec((bm, bn), lambda i, j: (i, j)),
        out_shape=jax.ShapeDtypeStruct((m, n), F32),
        compiler_params=_params("parallel", "parallel"),
        name="matmul_residual",
    )(a, w, res)


def _mm_groups_res_kernel(a0_ref, a1_ref, a2_ref, a3_ref, w_ref, r_ref, o_ref):
    kg = a0_ref.shape[1]
    acc = r_ref[...]
    for g, a_ref in enumerate((a0_ref, a1_ref, a2_ref, a3_ref)):
        acc = acc + _dot(a_ref[...], w_ref[g * kg:(g + 1) * kg, :].astype(BF16))
    o_ref[...] = acc


def _matmul_groups_residual(groups, w, layer, res, bm=1024, bn=512):
    m, kg = groups[0].shape
    _, k, n = w.shape
    bm, bn = _blk(m, bm), _blk(n, bn)
    a_spec = pl.BlockSpec((bm, kg), lambda i, j: (i, 0))
    return pl.pallas_call(
        _mm_groups_res_kernel,
        grid=(m // bm, n // bn),
        in_specs=[a_spec, a_spec, a_spec, a_spec,
                  pl.BlockSpec((None, k, bn), lambda i, j: (layer, 0, j)),
                  pl.BlockSpec((bm, bn), lambda i, j: (i, j))],
        out_specs=pl.BlockSpec((bm, bn), lambda i, j: (i, j)),
        out_shape=jax.ShapeDtypeStruct((m, n), F32),
        compiler_params=_params("parallel", "parallel"),
        name="matmul_groups_residual",
    )(*groups, w, res)


def _conv_kernel(h_ref, w_ref, b_ref, g_ref, beta_ref, o_ref, ybuf, cbuf, *, ts, c):
    s = pl.program_id(1)

    @pl.when(s == 0)
    def _():
        ybuf[0:CONV_HALO, :] = jnp.zeros((CONV_HALO, c), F32)

    @pl.when(s > 0)
    def _():
        ybuf[0:CONV_HALO, :] = ybuf[ts:ts + CONV_HALO, :]

    a = h_ref[0, :, 0:c]
    gate = h_ref[0, :, c:2 * c]
    ybuf[CONV_HALO:CONV_HALO + ts, :] = a * jax.nn.sigmoid(gate)

    row0 = CONV_HALO - (CONV_K - 1)
    rows = 32
    for cb in range(c // LANES):
        lanes = slice(cb * LANES, (cb + 1) * LANES)
        wcol = [w_ref[j:j + 1, lanes] for j in range(CONV_K)]
        bias = b_ref[:, lanes]
        for rb in range(ts // rows):
            acc = jnp.broadcast_to(bias, (rows, LANES))
            for j in range(CONV_K):
                r0 = rb * rows + row0 + j
                acc = acc + wcol[j] * ybuf[r0:r0 + rows, lanes]
            cbuf[rb * rows:(rb + 1) * rows, lanes] = acc

    y = cbuf[...]
    mu = jnp.mean(y, axis=-1, keepdims=True)
    d = y - mu
    var = jnp.mean(d * d, axis=-1, keepdims=True)
    y = d * lax.rsqrt(var + EPS) * g_ref[...] + beta_ref[...]
    o_ref[0] = jax.nn.silu(y).astype(o_ref.dtype)


def _conv_mixer(h, w_dw, b_dw, ln_g, ln_b, ts=128):
    b, s, c2 = h.shape
    c = c2 // 2
    vec = pl.BlockSpec((1, c), lambda bi, si: (0, 0))
    return pl.pallas_call(
        functools.partial(_conv_kernel, ts=ts, c=c),
        grid=(b, s // ts),
        in_specs=[pl.BlockSpec((1, ts, c2), lambda bi, si: (bi, si, 0)),
                  pl.BlockSpec((CONV_K, c), lambda bi, si: (0, 0)),
                  vec, vec, vec],
        out_specs=pl.BlockSpec((1, ts, c), lambda bi, si: (bi, si, 0)),
        out_shape=jax.ShapeDtypeStruct((b, s, c), BF16),
        scratch_shapes=[pltpu.VMEM((ts + CONV_HALO, c), F32), pltpu.VMEM((ts, c), F32)],
        compiler_params=_params("parallel", "arbitrary"),
        name="conv_mixer",
    )(h, w_dw, b_dw.reshape(1, c), ln_g.reshape(1, c), ln_b.reshape(1, c))


def _sgu_kernel(h_ref, g_ref, beta_ref, ws_ref, bs_ref, o_ref, *, ts, c):
    hh = jax.nn.gelu(h_ref[0])
    u = hh[:, 0:c]
    v = hh[:, c:2 * c]
    mu = jnp.mean(v, axis=-1, keepdims=True)
    d = v - mu
    var = jnp.mean(d * d, axis=-1, keepdims=True)
    v = (d * lax.rsqrt(var + EPS) * g_ref[...] + beta_ref[...]).astype(BF16)
    row = lax.broadcasted_iota(jnp.int32, (SGU_BLOCK, SGU_BLOCK), 0)
    col = lax.broadcasted_iota(jnp.int32, (SGU_BLOCK, SGU_BLOCK), 1)
    chunk_causal = (col // CHUNK) <= (row // CHUNK)
    for hd in range(c // SGU_HEAD):
        lanes = slice(hd * SGU_HEAD, (hd + 1) * SGU_HEAD)
        w = jnp.where(chunk_causal, ws_ref[hd], 0.0).astype(BF16)
        for blk in range(ts // SGU_BLOCK):
            rows = slice(blk * SGU_BLOCK, (blk + 1) * SGU_BLOCK)
            sv = _dot(w, v[rows, lanes]) + bs_ref[:, lanes]
            o_ref[0, rows, lanes] = (u[rows, lanes] * sv).astype(o_ref.dtype)


def _sgu_mixer(h, ln_g, ln_b, w_s, b_s, ts=256):
    b, s, c2 = h.shape
    c = c2 // 2
    heads = w_s.shape[0]
    bias = jnp.repeat(b_s.T, c // heads, axis=1)
    vec = pl.BlockSpec((1, c), lambda bi, si: (0, 0))
    return pl.pallas_call(
        functools.partial(_sgu_kernel, ts=ts, c=c),
        grid=(b, s // ts),
        in_specs=[pl.BlockSpec((1, ts, c2), lambda bi, si: (bi, si, 0)),
                  vec, vec,
                  pl.BlockSpec((heads, SGU_BLOCK, SGU_BLOCK), lambda bi, si: (0, 0, 0)),
                  pl.BlockSpec((SGU_BLOCK, c), lambda bi, si: (0, 0))],
        out_specs=pl.BlockSpec((1, ts, c), lambda bi, si: (bi, si, 0)),
        out_shape=jax.ShapeDtypeStruct((b, s, c), BF16),
        compiler_params=_params("parallel", "parallel"),
        name="sgu_mixer",
    )(h, ln_g.reshape(1, c), ln_b.reshape(1, c), w_s, bias)


def _softplus(z):
    return jnp.maximum(z, 0.0) + jnp.log1p(jnp.exp(-jnp.abs(z)))


def _sb_kernel(q_ref, k_ref, v_ref, o_ref, *, s, scale, heads):
    blk = SB_BLOCK
    row = lax.broadcasted_iota(jnp.int32, (blk, blk), 0)
    col = lax.broadcasted_iota(jnp.int32, (blk, blk), 1)
    before = col < row
    jrow = lax.broadcasted_iota(jnp.int32, (2 * blk, 2 * blk), 0) % blk
    scol = lax.broadcasted_iota(jnp.int32, (2 * blk, 2 * blk), 1)
    u2 = jnp.where((jrow > scol) | (scol >= blk), 1.0, 0.0).astype(BF16)

    def suffix(lk):
        hi = lk.astype(BF16)
        lo = (lk - hi.astype(F32)).astype(BF16)
        r = _dot(jnp.concatenate([hi, lo], axis=1), u2)
        return r[:, 0:blk], r[:, blk:2 * blk]

    head_lanes = [slice(hd * SB_HEAD, (hd + 1) * SB_HEAD) for hd in range(heads)]

    def qblock(qi, _):
        q0 = pl.multiple_of(qi * blk, blk)
        qbs = [q_ref[0, pl.ds(q0, blk), lanes] for lanes in head_lanes]

        def key_block(j0, st, diag):
            z = [lax.dot_general(qb, k_ref[0, pl.ds(j0, blk), lanes], _NT, preferred_element_type=F32) * scale
                 for qb, lanes in zip(qbs, head_lanes)]
            sp = [_softplus(x) for x in z]
            lk = [-x for x in sp]
            if diag:
                lk = [jnp.where(before, x, 0.0) for x in lk]
            suf = [suffix(x) for x in lk]
            wts = [jnp.exp((zz - ss) + (later + carry))
                   for zz, ss, (later, _), (carry, _) in zip(z, sp, suf, st)]
            if diag:
                wts = [jnp.where(before, x, 0.0) for x in wts]
            pv = [_dot(x.astype(BF16), v_ref[0, pl.ds(j0, blk), lanes]) for x, lanes in zip(wts, head_lanes)]
            return tuple((carry + total, acc + p) for (carry, acc), (_, total), p in zip(st, suf, pv))

        zeros = jnp.zeros((blk, blk), F32)
        st = key_block(q0, ((zeros, zeros),) * heads, True)

        def kstep(it, st):
            return key_block(pl.multiple_of((qi - 1 - it) * blk, blk), st, False)

        st = lax.fori_loop(0, qi, kstep, st)
        for hd, lanes in enumerate(head_lanes):
            o_ref[0, pl.ds(q0, blk), lanes] = st[hd][1].astype(o_ref.dtype)
        return 0

    lax.fori_loop(0, s // blk, qblock, 0)


def _sb_mixer(h, heads_per_step=8):
    b, s, c3 = h.shape
    c = c3 // 3
    w = heads_per_step * SB_HEAD
    groups = c // w
    return pl.pallas_call(
        functools.partial(_sb_kernel, s=s, scale=SB_HEAD ** -0.5, heads=heads_per_step),
        grid=(b, groups),
        in_specs=[pl.BlockSpec((1, s, w), lambda bi, gi: (bi, 0, gi)),
                  pl.BlockSpec((1, s, w), lambda bi, gi: (bi, 0, gi + groups)),
                  pl.BlockSpec((1, s, w), lambda bi, gi: (bi, 0, gi + 2 * groups))],
        out_specs=pl.BlockSpec((1, s, w), lambda bi, gi: (bi, 0, gi)),
        out_shape=jax.ShapeDtypeStruct((b, s, c), BF16),
        compiler_params=_params("parallel", "parallel"),
        name="sb_mixer",
    )(h, h, h)


def _rwkv_prep_kernel(h_ref, l_ref, mu_ref, mul_ref, w0_ref, wup_ref, a0_ref, aup_ref, gup_ref, kk_ref, ka_ref,
                      r_ref, lw_ref, kkr_ref, a_ref, kf_ref, v_ref, g_ref, hbuf, lbuf, *, ts, c):
    s = pl.program_id(1)
    c3 = 3 * c
    nl = l_ref.shape[2]

    @pl.when(s == 0)
    def _():
        hbuf[0:SUBLANES, :] = jnp.zeros((SUBLANES, c3), F32)
        lbuf[0:SUBLANES, :] = jnp.zeros((SUBLANES, nl), F32)

    @pl.when(s > 0)
    def _():
        hbuf[0:SUBLANES, :] = hbuf[ts:ts + SUBLANES, :]
        lbuf[0:SUBLANES, :] = lbuf[ts:ts + SUBLANES, :]

    h = h_ref[0]
    lo = l_ref[0]
    hbuf[SUBLANES:SUBLANES + ts, :] = h
    lbuf[SUBLANES:SUBLANES + ts, :] = lo
    h = h + (hbuf[SUBLANES - 1:SUBLANES - 1 + ts, :] - h) * mu_ref[...]
    lo = lo + (lbuf[SUBLANES - 1:SUBLANES - 1 + ts, :] - lo) * mul_ref[...]

    r = h[:, 0:c]
    k = h[:, c:2 * c]
    xw = lo[:, 0:LANES]
    xa = lo[:, LANES:2 * LANES]
    xg = lo[:, 2 * LANES:nl]

    w_log = -_softplus(-(w0_ref[...] + _dot(jnp.tanh(xw).astype(BF16), wup_ref[...]))) - 0.5
    a = jax.nn.sigmoid(a0_ref[...] + _dot(xa.astype(BF16), aup_ref[...]))
    r_ref[0] = r
    lw_ref[0] = -jnp.exp(w_log)
    kkr_ref[0] = k * kk_ref[...]
    a_ref[0] = a
    kf_ref[0] = k * (1.0 + (a - 1.0) * ka_ref[...])
    v_ref[0] = h[:, 2 * c:3 * c]
    g_ref[0] = _dot(jax.nn.sigmoid(xg).astype(BF16), gup_ref[...])


def _rwkv_prep(h, lora, mu, mul, w0, w_up, a0, a_up, g_up, k_k, k_a, ts=256):
    b, s, c3 = h.shape
    c = c3 // 3
    nl = lora.shape[2]
    nat = pl.BlockSpec((1, ts, c), lambda bi, si: (bi, si, 0))
    vec = lambda n: pl.BlockSpec((1, n), lambda bi, si: (0, 0))
    full = lambda a: pl.BlockSpec(a.shape, lambda bi, si: (0, 0))
    return pl.pallas_call(
        functools.partial(_rwkv_prep_kernel, ts=ts, c=c),
        grid=(b, s // ts),
        in_specs=[pl.BlockSpec((1, ts, c3), lambda bi, si: (bi, si, 0)),
                  pl.BlockSpec((1, ts, nl), lambda bi, si: (bi, si, 0)),
                  vec(c3), vec(nl), vec(c), full(w_up), vec(c), full(a_up), full(g_up), vec(c), vec(c)],
        out_specs=[nat] * 7,
        out_shape=[jax.ShapeDtypeStruct((b, s, c), F32)] * 7,
        scratch_shapes=[pltpu.VMEM((ts + SUBLANES, c3), F32), pltpu.VMEM((ts + SUBLANES, nl), F32)],
        compiler_params=_params("parallel", "arbitrary"),
        name="rwkv_prep",
    )(h, lora, mu.reshape(1, c3), mul.reshape(1, nl), w0.reshape(1, c), w_up, a0.reshape(1, c), a_up, g_up,
      k_k.reshape(1, c), k_a.reshape(1, c))


def _split(x):
    hi = x.astype(BF16)
    return hi, (x - hi.astype(F32)).astype(BF16)


def _dot3(a, b, dims=_NN):
    lhs = jnp.concatenate([a[0], a[1]], axis=1)
    zero = jnp.zeros_like(b[0])
    if dims is _NN:
        rhs = jnp.concatenate([jnp.concatenate([b[0], b[1]], axis=1),
                               jnp.concatenate([b[0], zero], axis=1)], axis=0)
    else:
        rhs = jnp.concatenate([jnp.concatenate([b[0], b[0]], axis=1),
                               jnp.concatenate([b[1], zero], axis=1)], axis=0)
    out = lax.dot_general(lhs, rhs, dims, preferred_element_type=F32)
    n = out.shape[1] // 2
    return out[:, :n] + out[:, n:]


def _rwkv_chunk_kernel(r_ref, lw_ref, kkr_ref, a_ref, kf_ref, v_ref, g_ref, rk_ref, lg_ref, lb_ref,
                       o_ref, state, *, tt, groups):
    n, t = RWKV_HEAD, RWKV_CHUNK
    t2 = 2 * t

    @pl.when(pl.program_id(2) == 0)
    def _():
        state[...] = jnp.zeros_like(state)

    first = lax.broadcasted_iota(jnp.int32, (t, LANES), 1) < n
    row = lax.broadcasted_iota(jnp.int32, (t2, t2), 0)
    col = lax.broadcasted_iota(jnp.int32, (t2, t2), 1)
    strict, incl = row > col, row >= col
    eye = jnp.where(row == col, 1.0, 0.0)
    tri3 = jnp.where(lax.broadcasted_iota(jnp.int32, (t, 3 * t), 0) >= lax.broadcasted_iota(jnp.int32, (t, 3 * t), 1) % t,
                     1.0, 0.0).astype(BF16)

    def head_sum(x):
        s0 = jnp.sum(jnp.where(first, x, 0.0), axis=1, keepdims=True)
        s1 = jnp.sum(jnp.where(first, 0.0, x), axis=1, keepdims=True)
        return jnp.where(first, s0, s1)

    def stack(x):
        return jnp.concatenate([jnp.where(first, x, 0.0), jnp.where(first, 0.0, x)], axis=0)

    cat = lambda *xs: jnp.concatenate(xs, axis=0)
    nch = tt // t
    units = [(slice(c * t, (c + 1) * t), slice(g * LANES, (g + 1) * LANES)) for g in range(groups) for c in range(nch)]
    each = lambda fn, *lists: [fn(*xs) for xs in zip(*lists)]

    def setup(rw, ln):
        r, lw, kkr, a, kf, v = (ref[0, rw, ln] for ref in (r_ref, lw_ref, kkr_ref, a_ref, kf_ref, v_ref))
        h1 = lw.astype(BF16)
        r1 = lw - h1.astype(F32)
        h2 = r1.astype(BF16)
        h3 = (r1 - h2.astype(F32)).astype(BF16)
        cum = _dot(tri3, cat(h1, h2, h3))
        cum_t = cum[t - 1:t, :]
        kk = kkr / jnp.maximum(jnp.sqrt(head_sum(kkr * kkr)), 1e-12)
        b = kk * a
        inv_p = jnp.exp(-cum)
        to_end = jnp.exp(cum_t - cum)
        ks = stack(kk * jnp.exp(cum - lw))
        kr = _split(cat(ks, stack(r * jnp.exp(cum))))
        bp = _split(cat(stack(b * inv_p), stack(kf * inv_p)))
        ends = _split(cat(stack(b * to_end), stack(kf * to_end)))
        bonus = head_sum(r * kf * rk_ref[:, ln]) * v
        return ks, kr, bp, ends, stack(v), jnp.exp(cum_t), bonus

    ks, kr, bp, ends, vs, decay_t, bonus = zip(*[setup(rw, ln) for rw, ln in units])
    gram = each(lambda x, y: _dot3(x, y, _NT), kr, bp)
    a_m = [jnp.where(strict, g[:t2, :t2], 0.0) for g in gram]
    catp = lambda x, y: (cat(x[0], y[0]), cat(x[1], y[1]))
    minv = [eye - x for x in a_m]
    pw = [_split(x) for x in a_m]
    pw = [_split(_dot3(x, x)) for x in pw]
    for _ in range(t.bit_length() - 3):
        both = each(lambda p, m: _dot3(catp(p, _split(m)), p), pw, minv)
        minv = each(lambda m, bo: m + bo[t2:], minv, both)
        pw = [_split(bo[:t2]) for bo in both]
    minv = each(lambda m, p: m + _dot3(_split(m), p), minv, pw)
    ms = [_split(m) for m in minv]
    vss = [_split(x) for x in vs]
    lv = each(lambda g, x: _dot3(_split(cat(jnp.where(strict, g[:t2, t2:], 0.0),
                                            jnp.where(incl, g[t2:, t2:], 0.0))), x), gram, vss)
    ds = [x[t2:] for x in lv]
    ktcs = each(lambda m, x, y: _dot3(m, _split(jnp.concatenate([x, y[:t2]], axis=1))), ms, ks, lv)
    kt = [_split(x[:, :LANES]) for x in ktcs]
    cs = [x[:, LANES:] for x in ktcs]
    lbs = [_split(jnp.where(incl, g[t2:, :t2], 0.0)) for g in gram]
    lhs = each(lambda k, x: (cat(k[0], x[0][t2:]), cat(k[1], x[1][t2:])), kt, kr)

    st = [state[g] for g in range(groups)]
    for c in range(nch):
        ids = [g * nch + c for g in range(groups)]
        x = [_dot3(lhs[i], _split(s_g), _NT) for i, s_g in zip(ids, st)]
        us = [-xx[:t2] - cs[i] for i, xx in zip(ids, x)]
        ys = [xx[t2:] + _dot3(lbs[i], _split(u)) + ds[i] for i, xx, u in zip(ids, x, us)]
        uv_t = [cat(u, vs[i]).T for i, u in zip(ids, us)]
        st = [s_g * decay_t[i] + _dot3(_split(m), ends[i]) for i, s_g, m in zip(ids, st, uv_t)]
        for i, yy in zip(ids, ys):
            rw, ln = units[i]
            y = yy[:t] + yy[t:]
            mu = head_sum(y) * (1.0 / n)
            d = y - mu
            var = head_sum(d * d) * (1.0 / n)
            y = d * lax.rsqrt(var + LNX_EPS) * lg_ref[:, ln] + lb_ref[:, ln]
            o_ref[0, rw, ln] = ((y + bonus[i]) * g_ref[0, rw, ln]).astype(o_ref.dtype)
    for g in range(groups):
        state[g] = st[g]


def _rwkv_chunks(r, lw, kkr, a, kf, v, g, r_k, lnx_g, lnx_b, tt=128, groups=4):
    b, s, c = v.shape
    tt = _blk(s, tt)
    cw = groups * LANES
    nat = pl.BlockSpec((1, tt, cw), lambda bi, pi, si: (bi, si, pi))
    vec = pl.BlockSpec((1, cw), lambda bi, pi, si: (0, pi))
    return pl.pallas_call(
        functools.partial(_rwkv_chunk_kernel, tt=tt, groups=groups),
        grid=(b, c // cw, s // tt),
        in_specs=[nat] * 7 + [vec] * 3,
        out_specs=nat,
        out_shape=jax.ShapeDtypeStruct((b, s, c), BF16),
        scratch_shapes=[pltpu.VMEM((groups, LANES, LANES), F32)],
        compiler_params=_params("parallel", "parallel", "arbitrary"),
        name="rwkv_chunks",
    )(r, lw, kkr, a, kf, v, g, r_k.reshape(1, c), lnx_g.reshape(1, c), lnx_b.reshape(1, c))


def _pad_rows(w, rows):
    return jnp.pad(w, ((0, rows - w.shape[0]), (0, 0)))


def _rwkv_mixer(h, lora, mu, w0, w_up, a0, a_up, g_up, k_k, k_a, r_k, lnx_g, lnx_b):
    c = h.shape[2] // 3
    mu_main = mu[:3 * c]
    mu_l = _lora_slots(mu[None, 3 * c:])[0]
    r, lw, kkr, a, kf, v, g = _rwkv_prep(
        h, lora, mu_main, mu_l, w0, _pad_rows(w_up, LANES).astype(BF16), a0,
        _pad_rows(a_up, LANES).astype(BF16), _pad_rows(g_up, 2 * LANES).astype(BF16), k_k, k_a)
    return _rwkv_chunks(r, lw, kkr, a, kf, v, g, r_k.reshape(-1), lnx_g, lnx_b)


def _lora_slots(w):
    xw = w[:, :DECAY_LORA]
    xa = w[:, DECAY_LORA:DECAY_LORA + ICLR_LORA]
    xg = w[:, DECAY_LORA + ICLR_LORA:]
    padto = lambda t, n: jnp.pad(t, ((0, 0), (0, n - t.shape[1])))
    return jnp.concatenate([padto(xw, LANES), padto(xa, LANES), padto(xg, 2 * LANES)], axis=1)


def _ffn(x2d, norm_g, w_in, w_out, layer):
    xn = _rmsnorm(x2d, norm_g, BF16)
    hid = _matmul_swiglu(xn, w_in, layer, BF16)
    return _matmul_residual(hid, w_out, layer, x2d, 0.5)


def kernel(x, ffn1_norm, ffn1_w_in, ffn1_w_out, mix_norm, mix_w_in, conv_w, conv_b, conv_ln_g, conv_ln_b, rwkv_mu, rwkv_w0, rwkv_w_up, rwkv_a0, rwkv_a_up, rwkv_g_up, rwkv_k_k, rwkv_k_a, rwkv_r_k, rwkv_lnx_g, rwkv_lnx_b, sgu_ln_g, sgu_ln_b, sgu_w_s, sgu_b_s, mix_w_out, ffn2_norm, ffn2_w_in, ffn2_w_out, final_norm):
    b, s, d = x.shape
    depth = ffn1_norm.shape[0]
    c = conv_w.shape[2]
    o_lora = 5 * c
    o_sgu = o_lora + DECAY_LORA + ICLR_LORA + GATE_LORA
    w_mix_t = jnp.swapaxes(mix_w_in, 1, 2)
    w_out1 = ffn1_w_out.astype(BF16)
    w_out2 = ffn2_w_out.astype(BF16)

    x2d = x.reshape(b * s, d)
    for l in range(depth):
        x2d = _ffn(x2d, ffn1_norm[l], ffn1_w_in, w_out1, l)

        xn = _rmsnorm(x2d, mix_norm[l], BF16)
        mm = lambda wt, lyr, row0, n, dt: _matmul_nt(xn, wt, lyr, row0, n, dt).reshape(b, s, n)
        w_lora_t = _lora_slots(mix_w_in[l, :, o_lora:o_sgu]).T[None]
        h_conv = mm(w_mix_t, l, 0, 2 * c, F32)
        h_rkv = mm(w_mix_t, l, 2 * c, 3 * c, F32)
        h_lora = mm(w_lora_t, 0, 0, w_lora_t.shape[1], F32)
        h_sgu = mm(w_mix_t, l, o_sgu, 2 * c, F32)
        h_sb = mm(w_mix_t, l, o_sgu + 2 * c, 3 * c, BF16)

        y_conv = _conv_mixer(h_conv, conv_w[l], conv_b[l], conv_ln_g[l], conv_ln_b[l])
        y_rwkv = _rwkv_mixer(h_rkv, h_lora, rwkv_mu[l], rwkv_w0[l], rwkv_w_up[l], rwkv_a0[l], rwkv_a_up[l],
                             rwkv_g_up[l], rwkv_k_k[l], rwkv_k_a[l], rwkv_r_k[l], rwkv_lnx_g[l], rwkv_lnx_b[l])
        y_sgu = _sgu_mixer(h_sgu, sgu_ln_g[l], sgu_ln_b[l], sgu_w_s[l], sgu_b_s[l])
        y_sb = _sb_mixer(h_sb)
        groups = [t.reshape(b * s, c) for t in (y_conv, y_rwkv, y_sgu, y_sb)]
        x2d = _matmul_groups_residual(groups, mix_w_out, l, x2d)

        x2d = _ffn(x2d, ffn2_norm[l], ffn2_w_in, w_out2, l)
    return _rmsnorm(x2d, final_norm, F32).reshape(b, s, d)
```

```python
import functools

import jax
import jax.numpy as jnp
from jax import lax
from jax.experimental import pallas as pl
from jax.experimental.pallas import tpu as pltpu

LANES = 128
SUBLANES = 8
VMEM_LIMIT_BYTES = 52 * 1024 * 1024

CHUNK = 64
CONV_K = 31
CONV_HALO = 32
RWKV_HEAD = 64
RWKV_CHUNK = 64
DECAY_LORA = 64
ICLR_LORA = 64
GATE_LORA = 160
LNX_EPS = 64e-5
SGU_BLOCK = 128
SGU_HEAD = 128
SB_HEAD = 128
SB_BLOCK = 128
EPS = 1e-6

BF16 = jnp.bfloat16
F32 = jnp.float32


def _params(*sem):
    return pltpu.CompilerParams(dimension_semantics=sem, vmem_limit_bytes=VMEM_LIMIT_BYTES)


_NN = (((1,), (0,)), ((), ()))
_NT = (((1,), (1,)), ((), ()))


def _dot(a, b):
    return jnp.dot(a, b, preferred_element_type=F32)


def _blk(n, pref):
    if n <= pref:
        return n
    for cand in range(pref - pref % LANES, 0, -LANES):
        if n % cand == 0:
            return cand
    return n


def _rmsnorm_kernel(x_ref, g_ref, o_ref):
    x = x_ref[...]
    ms = jnp.mean(x * x, axis=-1, keepdims=True)
    o_ref[...] = (x * lax.rsqrt(ms + EPS) * g_ref[...]).astype(o_ref.dtype)


def _rmsnorm(x, g, out_dtype, bm=256):
    m, d = x.shape
    bm = _blk(m, bm)
    return pl.pallas_call(
        _rmsnorm_kernel,
        grid=(m // bm,),
        in_specs=[pl.BlockSpec((bm, d), lambda i: (i, 0)),
                  pl.BlockSpec((1, d), lambda i: (0, 0))],
        out_specs=pl.BlockSpec((bm, d), lambda i: (i, 0)),
        out_shape=jax.ShapeDtypeStruct((m, d), out_dtype),
        compiler_params=_params("parallel"),
        name="rmsnorm",
    )(x, g.reshape(1, d))


def _mm_nt_kernel(a_ref, w_ref, o_ref):
    o_ref[...] = lax.dot_general(a_ref[...], w_ref[0].astype(BF16), _NT,
                                 preferred_element_type=F32).astype(o_ref.dtype)


def _matmul_nt(a, wt, layer, row0, n, out_dtype, bm=1024, bn=512):
    m, k = a.shape
    bm, bn = _blk(m, bm), _blk(n, bn)
    assert row0 % SUBLANES == 0
    w_spec = pl.BlockSpec((pl.Element(1), pl.Element(bn), pl.Element(k)),
                          lambda i, j: (layer, pl.multiple_of(row0 + j * bn, SUBLANES), 0))
    return pl.pallas_call(
        _mm_nt_kernel,
        grid=(m // bm, n // bn),
        in_specs=[pl.BlockSpec((bm, k), lambda i, j: (i, 0)), w_spec],
        out_specs=pl.BlockSpec((bm, bn), lambda i, j: (i, j)),
        out_shape=jax.ShapeDtypeStruct((m, n), out_dtype),
        compiler_params=_params("parallel", "parallel"),
        name="matmul_nt",
    )(a, wt)


def _mm_swiglu_kernel(a_ref, wg_ref, wu_ref, wo_ref, o_ref, wo_bf16_ref):
    a = a_ref[...]
    gate = _dot(a, wg_ref[...].astype(BF16))
    up = _dot(a, wu_ref[...].astype(BF16))
    o_ref[...] = (jax.nn.silu(gate) * up).astype(o_ref.dtype)
    wo_bf16_ref[...] = wo_ref[...].astype(BF16)


def _matmul_swiglu(a, w, w_out, layer, out_dtype, bm=2048, bn=256):
    m, k = a.shape
    f = w.shape[2] // 2
    bm, bn = _blk(m, bm), _blk(f, bn)
    nj = f // bn
    steps = (m // bm) * nj
    slab = w_out.shape[1] // steps
    assert slab * steps == w_out.shape[1] and slab % (2 * SUBLANES) == 0
    n = w_out.shape[2]
    return pl.pallas_call(
        _mm_swiglu_kernel,
        grid=(m // bm, nj),
        in_specs=[pl.BlockSpec((bm, k), lambda i, j: (i, 0), pipeline_mode=pl.Buffered(1)),
                  pl.BlockSpec((None, k, bn), lambda i, j: (layer, 0, j)),
                  pl.BlockSpec((None, k, bn), lambda i, j: (layer, 0, j + nj)),
                  pl.BlockSpec((None, slab, n), lambda i, j: (layer, i * nj + j, 0))],
        out_specs=[pl.BlockSpec((bm, bn), lambda i, j: (i, j)),
                   pl.BlockSpec((slab, n), lambda i, j: (i * nj + j, 0))],
        out_shape=[jax.ShapeDtypeStruct((m, f), out_dtype),
                   jax.ShapeDtypeStruct((w_out.shape[1], n), BF16)],
        compiler_params=_params("parallel", "arbitrary"),
        name="matmul_swiglu",
    )(a, w, w, w_out)


def _mm_res_kernel(a_ref, w_ref, r_ref, o_ref, *, scale):
    o_ref[...] = r_ref[...] + scale * _dot(a_ref[...], w_ref[...])


def _matmul_residual(a, w, res, scale, bm=512, bn=512):
    m, k = a.shape
    n = w.shape[1]
    bm, bn = _blk(m, bm), _blk(n, bn)
    return pl.pallas_call(
        functools.partial(_mm_res_kernel, scale=scale),
        grid=(m // bm, n // bn),
        in_specs=[pl.BlockSpec((bm, k), lambda i, j: (i, 0)),
                  pl.BlockSpec((k, bn), lambda i, j: (0, j)),
                  pl.BlockSpec((bm, bn), lambda i, j: (i, j))],
        out_specs=pl.BlockSpec((bm, bn), lambda i, j: (i, j)),
        out_shape=jax.ShapeDtypeStruct((m, n), F32),
        compiler_params=_params("parallel", "parallel"),
        name="matmul_residual",
    )(a, w, res)


def _mm_groups_res_kernel(a0_ref, a1_ref, a2_ref, a3_ref, w_ref, r_ref, o_ref):
    kg = a0_ref.shape[1]
    acc = r_ref[...]
    for g, a_ref in enumerate((a0_ref, a1_ref, a2_ref, a3_ref)):
        acc = acc + _dot(a_ref[...], w_ref[g * kg:(g + 1) * kg, :].astype(BF16))
    o_ref[...] = acc


def _matmul_groups_residual(groups, w, layer, res, bm=1024, bn=512):
    m, kg = groups[0].shape
    _, k, n = w.shape
    bm, bn = _blk(m, bm), _blk(n, bn)
    a_spec = pl.BlockSpec((bm, kg), lambda i, j: (i, 0))
    return pl.pallas_call(
        _mm_groups_res_kernel,
        grid=(m // bm, n // bn),
        in_specs=[a_spec, a_spec, a_spec, a_spec,
                  pl.BlockSpec((None, k, bn), lambda i, j: (layer, 0, j)),
                  pl.BlockSpec((bm, bn), lambda i, j: (i, j))],
        out_specs=pl.BlockSpec((bm, bn), lambda i, j: (i, j)),
        out_shape=jax.ShapeDtypeStruct((m, n), F32),
        compiler_params=_params("parallel", "parallel"),
        name="matmul_groups_residual",
    )(*groups, w, res)


def _conv_kernel(h_ref, w_ref, b_ref, g_ref, beta_ref, o_ref, ybuf, cbuf, *, ts, c):
    s = pl.program_id(1)

    @pl.when(s == 0)
    def _():
        ybuf[0:CONV_HALO, :] = jnp.zeros((CONV_HALO, c), F32)
        ybuf[CONV_HALO + ts:CONV_HALO + ts + SUBLANES, :] = jnp.zeros((SUBLANES, c), F32)

    @pl.when(s > 0)
    def _():
        ybuf[0:CONV_HALO, :] = ybuf[ts:ts + CONV_HALO, :]

    a = h_ref[0, :, 0:c]
    gate = h_ref[0, :, c:2 * c]
    ybuf[CONV_HALO:CONV_HALO + ts, :] = a * jax.nn.sigmoid(gate)

    row0 = CONV_HALO - (CONV_K - 1)
    span = ts + SUBLANES
    for cb in range(c // LANES):
        lanes = slice(cb * LANES, (cb + 1) * LANES)
        out = jnp.broadcast_to(b_ref[:, lanes], (ts, LANES))
        for p in range(SUBLANES):
            part = None
            for j in range(CONV_K):
                if (row0 + j) % SUBLANES == p:
                    r0 = row0 + j - p
                    term = w_ref[j:j + 1, lanes] * ybuf[r0:r0 + span, lanes]
                    part = term if part is None else part + term
            out = out + part[p:p + ts, :]
        cbuf[:, lanes] = out

    y = cbuf[...]
    mu = jnp.mean(y, axis=-1, keepdims=True)
    d = y - mu
    var = jnp.mean(d * d, axis=-1, keepdims=True)
    y = d * lax.rsqrt(var + EPS) * g_ref[...] + beta_ref[...]
    o_ref[0] = jax.nn.silu(y).astype(o_ref.dtype)


def _conv_mixer(h, w_dw, b_dw, ln_g, ln_b, ts=128):
    b, s, c2 = h.shape
    c = c2 // 2
    vec = pl.BlockSpec((1, c), lambda bi, si: (0, 0))
    return pl.pallas_call(
        functools.partial(_conv_kernel, ts=ts, c=c),
        grid=(b, s // ts),
        in_specs=[pl.BlockSpec((1, ts, c2), lambda bi, si: (bi, si, 0)),
                  pl.BlockSpec((CONV_K, c), lambda bi, si: (0, 0)),
                  vec, vec, vec],
        out_specs=pl.BlockSpec((1, ts, c), lambda bi, si: (bi, si, 0)),
        out_shape=jax.ShapeDtypeStruct((b, s, c), BF16),
        scratch_shapes=[pltpu.VMEM((ts + CONV_HALO + SUBLANES, c), F32), pltpu.VMEM((ts, c), F32)],
        compiler_params=_params("parallel", "arbitrary"),
        name="conv_mixer",
    )(h, w_dw, b_dw.reshape(1, c), ln_g.reshape(1, c), ln_b.reshape(1, c))


def _sgu_kernel(h_ref, g_ref, beta_ref, ws_ref, bs_ref, o_ref, *, ts, c):
    hh = jax.nn.gelu(h_ref[0])
    u = hh[:, 0:c]
    v = hh[:, c:2 * c]
    mu = jnp.mean(v, axis=-1, keepdims=True)
    d = v - mu
    var = jnp.mean(d * d, axis=-1, keepdims=True)
    v = (d * lax.rsqrt(var + EPS) * g_ref[...] + beta_ref[...]).astype(BF16)
    row = lax.broadcasted_iota(jnp.int32, (SGU_BLOCK, SGU_BLOCK), 0)
    col = lax.broadcasted_iota(jnp.int32, (SGU_BLOCK, SGU_BLOCK), 1)
    chunk_causal = (col // CHUNK) <= (row // CHUNK)
    for hd in range(c // SGU_HEAD):
        lanes = slice(hd * SGU_HEAD, (hd + 1) * SGU_HEAD)
        w = jnp.where(chunk_causal, ws_ref[hd], 0.0).astype(BF16)
        for blk in range(ts // SGU_BLOCK):
            rows = slice(blk * SGU_BLOCK, (blk + 1) * SGU_BLOCK)
            sv = _dot(w, v[rows, lanes]) + bs_ref[:, lanes]
            o_ref[0, rows, lanes] = (u[rows, lanes] * sv).astype(o_ref.dtype)


def _sgu_mixer(h, ln_g, ln_b, w_s, b_s, ts=256):
    b, s, c2 = h.shape
    c = c2 // 2
    heads = w_s.shape[0]
    bias = jnp.repeat(b_s.T, c // heads, axis=1)
    vec = pl.BlockSpec((1, c), lambda bi, si: (0, 0))
    return pl.pallas_call(
        functools.partial(_sgu_kernel, ts=ts, c=c),
        grid=(b, s // ts),
        in_specs=[pl.BlockSpec((1, ts, c2), lambda bi, si: (bi, si, 0)),
                  vec, vec,
                  pl.BlockSpec((heads, SGU_BLOCK, SGU_BLOCK), lambda bi, si: (0, 0, 0)),
                  pl.BlockSpec((SGU_BLOCK, c), lambda bi, si: (0, 0))],
        out_specs=pl.BlockSpec((1, ts, c), lambda bi, si: (bi, si, 0)),
        out_shape=jax.ShapeDtypeStruct((b, s, c), BF16),
        compiler_params=_params("parallel", "parallel"),
        name="sgu_mixer",
    )(h, ln_g.reshape(1, c), ln_b.reshape(1, c), w_s, bias)


def _softplus(z):
    return jnp.maximum(z, 0.0) + jnp.log1p(jnp.exp(-jnp.abs(z)))


def _sb_kernel(q_ref, k_ref, v_ref, o_ref, *, s, scale, heads):
    blk = SB_BLOCK
    row = lax.broadcasted_iota(jnp.int32, (blk, blk), 0)
    col = lax.broadcasted_iota(jnp.int32, (blk, blk), 1)
    before = col < row
    jrow = lax.broadcasted_iota(jnp.int32, (2 * blk, 2 * blk), 0) % blk
    scol = lax.broadcasted_iota(jnp.int32, (2 * blk, 2 * blk), 1)
    u2 = jnp.where((jrow > scol) | (scol >= blk), 1.0, 0.0).astype(BF16)

    def suffix(lk):
        hi = lk.astype(BF16)
        lo = (lk - hi.astype(F32)).astype(BF16)
        r = _dot(jnp.concatenate([hi, lo], axis=1), u2)
        return r[:, 0:blk], r[:, blk:2 * blk]

    head_lanes = [slice(hd * SB_HEAD, (hd + 1) * SB_HEAD) for hd in range(heads)]

    def qblock(qi, _):
        q0 = pl.multiple_of(qi * blk, blk)
        qbs = [q_ref[0, pl.ds(q0, blk), lanes] for lanes in head_lanes]

        def key_block(j0, st, diag):
            z = [lax.dot_general(qb, k_ref[0, pl.ds(j0, blk), lanes], _NT, preferred_element_type=F32) * scale
                 for qb, lanes in zip(qbs, head_lanes)]
            sp = [_softplus(x) for x in z]
            lk = [-x for x in sp]
            if diag:
                lk = [jnp.where(before, x, 0.0) for x in lk]
            suf = [suffix(x) for x in lk]
            wts = [jnp.exp((zz - ss) + (later + carry))
                   for zz, ss, (later, _), (carry, _) in zip(z, sp, suf, st)]
            if diag:
                wts = [jnp.where(before, x, 0.0) for x in wts]
            pv = [_dot(x.astype(BF16), v_ref[0, pl.ds(j0, blk), lanes]) for x, lanes in zip(wts, head_lanes)]
            return tuple((carry + total, acc + p) for (carry, acc), (_, total), p in zip(st, suf, pv))

        zeros = jnp.zeros((blk, blk), F32)
        st = key_block(q0, ((zeros, zeros),) * heads, True)

        def kstep(it, st):
            return key_block(pl.multiple_of((qi - 1 - it) * blk, blk), st, False)

        st = lax.fori_loop(0, qi, kstep, st)
        for hd, lanes in enumerate(head_lanes):
            o_ref[0, pl.ds(q0, blk), lanes] = st[hd][1].astype(o_ref.dtype)
        return 0

    lax.fori_loop(0, s // blk, qblock, 0)


def _sb_mixer(h, heads_per_step=8):
    b, s, c3 = h.shape
    c = c3 // 3
    w = heads_per_step * SB_HEAD
    groups = c // w
    return pl.pallas_call(
        functools.partial(_sb_kernel, s=s, scale=SB_HEAD ** -0.5, heads=heads_per_step),
        grid=(b, groups),
        in_specs=[pl.BlockSpec((1, s, w), lambda bi, gi: (bi, 0, gi)),
                  pl.BlockSpec((1, s, w), lambda bi, gi: (bi, 0, gi + groups)),
                  pl.BlockSpec((1, s, w), lambda bi, gi: (bi, 0, gi + 2 * groups))],
        out_specs=pl.BlockSpec((1, s, w), lambda bi, gi: (bi, 0, gi)),
        out_shape=jax.ShapeDtypeStruct((b, s, c), BF16),
        compiler_params=_params("parallel", "parallel"),
        name="sb_mixer",
    )(h, h, h)


def _rwkv_prep_kernel(h_ref, l_ref, mu_ref, mul_ref, w0_ref, wup_ref, a0_ref, aup_ref, gup_ref, kk_ref, ka_ref,
                      r_ref, lw_ref, kkr_ref, a_ref, kf_ref, v_ref, g_ref, hbuf, lbuf, *, ts, c):
    s = pl.program_id(1)
    c3 = 3 * c
    nl = l_ref.shape[2]

    @pl.when(s == 0)
    def _():
        hbuf[0:SUBLANES, :] = jnp.zeros((SUBLANES, c3), F32)
        lbuf[0:SUBLANES, :] = jnp.zeros((SUBLANES, nl), F32)

    @pl.when(s > 0)
    def _():
        hbuf[0:SUBLANES, :] = hbuf[ts:ts + SUBLANES, :]
        lbuf[0:SUBLANES, :] = lbuf[ts:ts + SUBLANES, :]

    h = h_ref[0]
    lo = l_ref[0]
    hbuf[SUBLANES:SUBLANES + ts, :] = h
    lbuf[SUBLANES:SUBLANES + ts, :] = lo
    h = h + (hbuf[SUBLANES - 1:SUBLANES - 1 + ts, :] - h) * mu_ref[...]
    lo = lo + (lbuf[SUBLANES - 1:SUBLANES - 1 + ts, :] - lo) * mul_ref[...]

    r = h[:, 0:c]
    k = h[:, c:2 * c]
    xw = lo[:, 0:LANES]
    xa = lo[:, LANES:2 * LANES]
    xg = lo[:, 2 * LANES:nl]

    w_log = -_softplus(-(w0_ref[...] + _dot(jnp.tanh(xw).astype(BF16), wup_ref[...]))) - 0.5
    a = jax.nn.sigmoid(a0_ref[...] + _dot(xa.astype(BF16), aup_ref[...]))
    r_ref[0] = r
    lw_ref[0] = -jnp.exp(w_log)
    kkr_ref[0] = k * kk_ref[...]
    a_ref[0] = a
    kf_ref[0] = k * (1.0 + (a - 1.0) * ka_ref[...])
    v_ref[0] = h[:, 2 * c:3 * c]
    g_ref[0] = _dot(jax.nn.sigmoid(xg).astype(BF16), gup_ref[...])


def _rwkv_prep(h, lora, mu, mul, w0, w_up, a0, a_up, g_up, k_k, k_a, ts=256):
    b, s, c3 = h.shape
    c = c3 // 3
    nl = lora.shape[2]
    nat = pl.BlockSpec((1, ts, c), lambda bi, si: (bi, si, 0))
    vec = lambda n: pl.BlockSpec((1, n), lambda bi, si: (0, 0))
    full = lambda a: pl.BlockSpec(a.shape, lambda bi, si: (0, 0))
    return pl.pallas_call(
        functools.partial(_rwkv_prep_kernel, ts=ts, c=c),
        grid=(b, s // ts),
        in_specs=[pl.BlockSpec((1, ts, c3), lambda bi, si: (bi, si, 0)),
                  pl.BlockSpec((1, ts, nl), lambda bi, si: (bi, si, 0)),
                  vec(c3), vec(nl), vec(c), full(w_up), vec(c), full(a_up), full(g_up), vec(c), vec(c)],
        out_specs=[nat] * 7,
        out_shape=[jax.ShapeDtypeStruct((b, s, c), F32)] * 7,
        scratch_shapes=[pltpu.VMEM((ts + SUBLANES, c3), F32), pltpu.VMEM((ts + SUBLANES, nl), F32)],
        compiler_params=_params("parallel", "arbitrary"),
        name="rwkv_prep",
    )(h, lora, mu.reshape(1, c3), mul.reshape(1, nl), w0.reshape(1, c), w_up, a0.reshape(1, c), a_up, g_up,
      k_k.reshape(1, c), k_a.reshape(1, c))


def _split(x):
    hi = x.astype(BF16)
    return hi, (x - hi.astype(F32)).astype(BF16)


def _dot3(a, b, dims=_NN):
    lhs = jnp.concatenate([a[0], a[1]], axis=1)
    zero = jnp.zeros_like(b[0])
    if dims is _NN:
        rhs = jnp.concatenate([jnp.concatenate([b[0], b[1]], axis=1),
                               jnp.concatenate([b[0], zero], axis=1)], axis=0)
    else:
        rhs = jnp.concatenate([jnp.concatenate([b[0], b[0]], axis=1),
                               jnp.concatenate([b[1], zero], axis=1)], axis=0)
    out = lax.dot_general(lhs, rhs, dims, preferred_element_type=F32)
    n = out.shape[1] // 2
    return out[:, :n] + out[:, n:]


def _rwkv_chunk_kernel(r_ref, lw_ref, kkr_ref, a_ref, kf_ref, v_ref, g_ref, rk_ref, lg_ref, lb_ref,
                       o_ref, state, *, tt, groups):
    n, t = RWKV_HEAD, RWKV_CHUNK
    t2 = 2 * t

    @pl.when(pl.program_id(2) == 0)
    def _():
        state[...] = jnp.zeros_like(state)

    first = lax.broadcasted_iota(jnp.int32, (t, LANES), 1) < n
    row = lax.broadcasted_iota(jnp.int32, (t2, t2), 0)
    col = lax.broadcasted_iota(jnp.int32, (t2, t2), 1)
    strict, incl = row > col, row >= col
    eye = jnp.where(row == col, 1.0, 0.0)
    tri3 = jnp.where(lax.broadcasted_iota(jnp.int32, (t, 3 * t), 0) >= lax.broadcasted_iota(jnp.int32, (t, 3 * t), 1) % t,
                     1.0, 0.0).astype(BF16)

    def head_sum(x):
        s0 = jnp.sum(jnp.where(first, x, 0.0), axis=1, keepdims=True)
        s1 = jnp.sum(jnp.where(first, 0.0, x), axis=1, keepdims=True)
        return jnp.where(first, s0, s1)

    def stack(x):
        return jnp.concatenate([jnp.where(first, x, 0.0), jnp.where(first, 0.0, x)], axis=0)

    cat = lambda *xs: jnp.concatenate(xs, axis=0)
    nch = tt // t
    units = [(slice(c * t, (c + 1) * t), slice(g * LANES, (g + 1) * LANES)) for g in range(groups) for c in range(nch)]
    each = lambda fn, *lists: [fn(*xs) for xs in zip(*lists)]

    def setup(rw, ln):
        r, lw, kkr, a, kf, v = (ref[0, rw, ln] for ref in (r_ref, lw_ref, kkr_ref, a_ref, kf_ref, v_ref))
        h1 = lw.astype(BF16)
        r1 = lw - h1.astype(F32)
        h2 = r1.astype(BF16)
        h3 = (r1 - h2.astype(F32)).astype(BF16)
        cum = _dot(tri3, cat(h1, h2, h3))
        cum_t = cum[t - 1:t, :]
        kk = kkr / jnp.maximum(jnp.sqrt(head_sum(kkr * kkr)), 1e-12)
        b = kk * a
        inv_p = jnp.exp(-cum)
        to_end = jnp.exp(cum_t - cum)
        ks = stack(kk * jnp.exp(cum - lw))
        kr = _split(cat(ks, stack(r * jnp.exp(cum))))
        bp = _split(cat(stack(b * inv_p), stack(kf * inv_p)))
        ends = _split(cat(stack(b * to_end), stack(kf * to_end)))
        bonus = head_sum(r * kf * rk_ref[:, ln]) * v
        return ks, kr, bp, ends, stack(v), jnp.exp(cum_t), bonus

    ks, kr, bp, ends, vs, decay_t, bonus = zip(*[setup(rw, ln) for rw, ln in units])
    gram = each(lambda x, y: _dot3(x, y, _NT), kr, bp)
    a_m = [jnp.where(strict, g[:t2, :t2], 0.0) for g in gram]
    catp = lambda x, y: (cat(x[0], y[0]), cat(x[1], y[1]))
    minv = [eye - x for x in a_m]
    pw = [_split(x) for x in a_m]
    pw = [_split(_dot3(x, x)) for x in pw]
    for _ in range(t.bit_length() - 3):
        both = each(lambda p, m: _dot3(catp(p, _split(m)), p), pw, minv)
        minv = each(lambda m, bo: m + bo[t2:], minv, both)
        pw = [_split(bo[:t2]) for bo in both]
    minv = each(lambda m, p: m + _dot3(_split(m), p), minv, pw)
    ms = [_split(m) for m in minv]
    vss = [_split(x) for x in vs]
    lv = each(lambda g, x: _dot3(_split(cat(jnp.where(strict, g[:t2, t2:], 0.0),
                                            jnp.where(incl, g[t2:, t2:], 0.0))), x), gram, vss)
    ds = [x[t2:] for x in lv]
    ktcs = each(lambda m, x, y: _dot3(m, _split(jnp.concatenate([x, y[:t2]], axis=1))), ms, ks, lv)
    kt = [_split(x[:, :LANES]) for x in ktcs]
    cs = [x[:, LANES:] for x in ktcs]
    lbs = [_split(jnp.where(incl, g[t2:, :t2], 0.0)) for g in gram]
    lhs = each(lambda k, x: (cat(k[0], x[0][t2:]), cat(k[1], x[1][t2:])), kt, kr)

    st = [state[g] for g in range(groups)]
    for c in range(nch):
        ids = [g * nch + c for g in range(groups)]
        x = [_dot3(lhs[i], _split(s_g), _NT) for i, s_g in zip(ids, st)]
        us = [-xx[:t2] - cs[i] for i, xx in zip(ids, x)]
        ys = [xx[t2:] + _dot3(lbs[i], _split(u)) + ds[i] for i, xx, u in zip(ids, x, us)]
        uv_t = [cat(u, vs[i]).T for i, u in zip(ids, us)]
        st = [s_g * decay_t[i] + _dot3(_split(m), ends[i]) for i, s_g, m in zip(ids, st, uv_t)]
        for i, yy in zip(ids, ys):
            rw, ln = units[i]
            y = yy[:t] + yy[t:]
            mu = head_sum(y) * (1.0 / n)
            d = y - mu
            var = head_sum(d * d) * (1.0 / n)
            y = d * lax.rsqrt(var + LNX_EPS) * lg_ref[:, ln] + lb_ref[:, ln]
            o_ref[0, rw, ln] = ((y + bonus[i]) * g_ref[0, rw, ln]).astype(o_ref.dtype)
    for g in range(groups):
        state[g] = st[g]


def _rwkv_chunks(r, lw, kkr, a, kf, v, g, r_k, lnx_g, lnx_b, tt=128, groups=4):
    b, s, c = v.shape
    tt = _blk(s, tt)
    cw = groups * LANES
    nat = pl.BlockSpec((1, tt, cw), lambda bi, pi, si: (bi, si, pi))
    vec = pl.BlockSpec((1, cw), lambda bi, pi, si: (0, pi))
    return pl.pallas_call(
        functools.partial(_rwkv_chunk_kernel, tt=tt, groups=groups),
        grid=(b, c // cw, s // tt),
        in_specs=[nat] * 7 + [vec] * 3,
        out_specs=nat,
        out_shape=jax.ShapeDtypeStruct((b, s, c), BF16),
        scratch_shapes=[pltpu.VMEM((groups, LANES, LANES), F32)],
        compiler_params=_params("parallel", "parallel", "arbitrary"),
        name="rwkv_chunks",
    )(r, lw, kkr, a, kf, v, g, r_k.reshape(1, c), lnx_g.reshape(1, c), lnx_b.reshape(1, c))


def _pad_rows(w, rows):
    return jnp.pad(w, ((0, rows - w.shape[0]), (0, 0)))


def _rwkv_mixer(h, lora, mu, w0, w_up, a0, a_up, g_up, k_k, k_a, r_k, lnx_g, lnx_b):
    c = h.shape[2] // 3
    mu_main = mu[:3 * c]
    mu_l = _lora_slots(mu[None, 3 * c:])[0]
    r, lw, kkr, a, kf, v, g = _rwkv_prep(
        h, lora, mu_main, mu_l, w0, _pad_rows(w_up, LANES).astype(BF16), a0,
        _pad_rows(a_up, LANES).astype(BF16), _pad_rows(g_up, 2 * LANES).astype(BF16), k_k, k_a)
    return _rwkv_chunks(r, lw, kkr, a, kf, v, g, r_k.reshape(-1), lnx_g, lnx_b)


def _lora_slots(w):
    xw = w[:, :DECAY_LORA]
    xa = w[:, DECAY_LORA:DECAY_LORA + ICLR_LORA]
    xg = w[:, DECAY_LORA + ICLR_LORA:]
    padto = lambda t, n: jnp.pad(t, ((0, 0), (0, n - t.shape[1])))
    return jnp.concatenate([padto(xw, LANES), padto(xa, LANES), padto(xg, 2 * LANES)], axis=1)


def _ffn(x2d, norm_g, w_in, w_out, layer):
    xn = _rmsnorm(x2d, norm_g, BF16)
    hid, w_out_bf16 = _matmul_swiglu(xn, w_in, w_out, layer, BF16)
    return _matmul_residual(hid, w_out_bf16, x2d, 0.5)


def kernel(x, ffn1_norm, ffn1_w_in, ffn1_w_out, mix_norm, mix_w_in, conv_w, conv_b, conv_ln_g, conv_ln_b, rwkv_mu, rwkv_w0, rwkv_w_up, rwkv_a0, rwkv_a_up, rwkv_g_up, rwkv_k_k, rwkv_k_a, rwkv_r_k, rwkv_lnx_g, rwkv_lnx_b, sgu_ln_g, sgu_ln_b, sgu_w_s, sgu_b_s, mix_w_out, ffn2_norm, ffn2_w_in, ffn2_w_out, final_norm):
    b, s, d = x.shape
    depth = ffn1_norm.shape[0]
    c = conv_w.shape[2]
    o_lora = 5 * c
    o_sgu = o_lora + DECAY_LORA + ICLR_LORA + GATE_LORA
    w_mix_t = jnp.swapaxes(mix_w_in, 1, 2)

    x2d = x.reshape(b * s, d)
    for l in range(depth):
        x2d = _ffn(x2d, ffn1_norm[l], ffn1_w_in, ffn1_w_out, l)

        xn = _rmsnorm(x2d, mix_norm[l], BF16)
        mm = lambda wt, lyr, row0, n, dt: _matmul_nt(xn, wt, lyr, row0, n, dt).reshape(b, s, n)
        w_lora_t = _lora_slots(mix_w_in[l, :, o_lora:o_sgu]).T[None]
        h_conv = mm(w_mix_t, l, 0, 2 * c, F32)
        h_rkv = mm(w_mix_t, l, 2 * c, 3 * c, F32)
        h_lora = mm(w_lora_t, 0, 0, w_lora_t.shape[1], F32)
        h_sgu = mm(w_mix_t, l, o_sgu, 2 * c, F32)
        h_sb = mm(w_mix_t, l, o_sgu + 2 * c, 3 * c, BF16)

        y_conv = _conv_mixer(h_conv, conv_w[l], conv_b[l], conv_ln_g[l], conv_ln_b[l])
        y_rwkv = _rwkv_mixer(h_rkv, h_lora, rwkv_mu[l], rwkv_w0[l], rwkv_w_up[l], rwkv_a0[l], rwkv_a_up[l],
                             rwkv_g_up[l], rwkv_k_k[l], rwkv_k_a[l], rwkv_r_k[l], rwkv_lnx_g[l], rwkv_lnx_b[l])
        y_sgu = _sgu_mixer(h_sgu, sgu_ln_g[l], sgu_ln_b[l], sgu_w_s[l], sgu_b_s[l])
        y_sb = _sb_mixer(h_sb)
        groups = [t.reshape(b * s, c) for t in (y_conv, y_rwkv, y_sgu, y_sb)]
        x2d = _matmul_groups_residual(groups, mix_w_out, l, x2d)

        x2d = _ffn(x2d, ffn2_norm[l], ffn2_w_in, ffn2_w_out, l)
    return _rmsnorm(x2d, final_norm, F32).reshape(b, s, d)
```

```python
import functools

import jax
import jax.numpy as jnp
from jax import lax
from jax.experimental import pallas as pl
from jax.experimental.pallas import tpu as pltpu

LANES = 128
SUBLANES = 8
VMEM_LIMIT_BYTES = 52 * 1024 * 1024
VMEM_LIMIT_WIDE_BYTES = 57 * 1024 * 1024

CHUNK = 64
CONV_K = 31
CONV_HALO = 32
RWKV_HEAD = 64
RWKV_CHUNK = 64
DECAY_LORA = 64
ICLR_LORA = 64
GATE_LORA = 160
LNX_EPS = 64e-5
SGU_BLOCK = 128
SGU_HEAD = 128
SB_HEAD = 128
SB_BLOCK = 128
EPS = 1e-6

BF16 = jnp.bfloat16
F32 = jnp.float32


def _params(*sem, vmem=VMEM_LIMIT_BYTES):
    return pltpu.CompilerParams(dimension_semantics=sem, vmem_limit_bytes=vmem)


_NN = (((1,), (0,)), ((), ()))
_NT = (((1,), (1,)), ((), ()))


def _dot(a, b):
    return jnp.dot(a, b, preferred_element_type=F32)


def _blk(n, pref):
    if n <= pref:
        return n
    for cand in range(pref - pref % LANES, 0, -LANES):
        if n % cand == 0:
            return cand
    return n


def _rmsnorm_kernel(x_ref, g_ref, o_ref):
    x = x_ref[...]
    ms = jnp.mean(x * x, axis=-1, keepdims=True)
    o_ref[...] = (x * lax.rsqrt(ms + EPS) * g_ref[...]).astype(o_ref.dtype)


def _rmsnorm(x, g, out_dtype, bm=256):
    m, d = x.shape
    bm = _blk(m, bm)
    return pl.pallas_call(
        _rmsnorm_kernel,
        grid=(m // bm,),
        in_specs=[pl.BlockSpec((bm, d), lambda i: (i, 0)),
                  pl.BlockSpec((1, d), lambda i: (0, 0))],
        out_specs=pl.BlockSpec((bm, d), lambda i: (i, 0)),
        out_shape=jax.ShapeDtypeStruct((m, d), out_dtype),
        compiler_params=_params("parallel"),
        name="rmsnorm",
    )(x, g.reshape(1, d))


def _mm_nt_kernel(a_ref, w_ref, o_ref):
    o_ref[...] = lax.dot_general(a_ref[...], w_ref[0].astype(BF16), _NT,
                                 preferred_element_type=F32).astype(o_ref.dtype)


def _matmul_nt(a, wt, layer, row0, n, out_dtype, bm=2048, bn=256):
    m, k = a.shape
    bm, bn = _blk(m, bm), _blk(n, bn)
    assert row0 % SUBLANES == 0
    w_spec = pl.BlockSpec((pl.Element(1), pl.Element(bn), pl.Element(k)),
                          lambda i, j: (layer, pl.multiple_of(row0 + j * bn, SUBLANES), 0))
    return pl.pallas_call(
        _mm_nt_kernel,
        grid=(m // bm, n // bn),
        in_specs=[pl.BlockSpec((bm, k), lambda i, j: (i, 0)), w_spec],
        out_specs=pl.BlockSpec((bm, bn), lambda i, j: (i, j)),
        out_shape=jax.ShapeDtypeStruct((m, n), out_dtype),
        compiler_params=_params("parallel", "parallel", vmem=VMEM_LIMIT_WIDE_BYTES),
        name="matmul_nt",
    )(a, wt)


def _mm_swiglu_kernel(a_ref, wg_ref, wu_ref, wo_ref, o_ref, wo_bf16_ref):
    a = a_ref[...]
    gate = _dot(a, wg_ref[...].astype(BF16))
    up = _dot(a, wu_ref[...].astype(BF16))
    o_ref[...] = (jax.nn.silu(gate) * up).astype(o_ref.dtype)
    wo_bf16_ref[...] = wo_ref[...].astype(BF16)


def _matmul_swiglu(a, w, w_out, layer, out_dtype, bm=2048, bn=256):
    m, k = a.shape
    f = w.shape[2] // 2
    bm, bn = _blk(m, bm), _blk(f, bn)
    nj = f // bn
    steps = (m // bm) * nj
    slab = w_out.shape[1] // steps
    assert slab * steps == w_out.shape[1] and slab % (2 * SUBLANES) == 0
    n = w_out.shape[2]
    return pl.pallas_call(
        _mm_swiglu_kernel,
        grid=(m // bm, nj),
        in_specs=[pl.BlockSpec((bm, k), lambda i, j: (i, 0), pipeline_mode=pl.Buffered(1)),
                  pl.BlockSpec((None, k, bn), lambda i, j: (layer, 0, j)),
                  pl.BlockSpec((None, k, bn), lambda i, j: (layer, 0, j + nj)),
                  pl.BlockSpec((None, slab, n), lambda i, j: (layer, i * nj + j, 0))],
        out_specs=[pl.BlockSpec((bm, bn), lambda i, j: (i, j)),
                   pl.BlockSpec((slab, n), lambda i, j: (i * nj + j, 0))],
        out_shape=[jax.ShapeDtypeStruct((m, f), out_dtype),
                   jax.ShapeDtypeStruct((w_out.shape[1], n), BF16)],
        compiler_params=_params("parallel", "arbitrary"),
        name="matmul_swiglu",
    )(a, w, w, w_out)


def _mm_res_kernel(a_ref, w_ref, r_ref, o_ref, *, scale):
    o_ref[...] = r_ref[...] + scale * _dot(a_ref[...], w_ref[...])


def _matmul_residual(a, w, res, scale, bm=512, bn=512):
    m, k = a.shape
    n = w.shape[1]
    bm, bn = _blk(m, bm), _blk(n, bn)
    return pl.pallas_call(
        functools.partial(_mm_res_kernel, scale=scale),
        grid=(m // bm, n // bn),
        in_specs=[pl.BlockSpec((bm, k), lambda i, j: (i, 0)),
                  pl.BlockSpec((k, bn), lambda i, j: (0, j)),
                  pl.BlockSpec((bm, bn), lambda i, j: (i, j))],
        out_specs=pl.BlockSpec((bm, bn), lambda i, j: (i, j)),
        out_shape=jax.ShapeDtypeStruct((m, n), F32),
        compiler_params=_params("parallel", "parallel"),
        name="matmul_residual",
    )(a, w, res)


def _mm_groups_res_kernel(a0_ref, a1_ref, a2_ref, a3_ref, w_ref, r_ref, o_ref):
    kg = a0_ref.shape[1]
    acc = r_ref[...]
    for g, a_ref in enumerate((a0_ref, a1_ref, a2_ref, a3_ref)):
        acc = acc + _dot(a_ref[...], w_ref[g * kg:(g + 1) * kg, :].astype(BF16))
    o_ref[...] = acc


def _matmul_groups_residual(groups, w, layer, res, bm=1024, bn=512):
    m, kg = groups[0].shape
    _, k, n = w.shape
    bm, bn = _blk(m, bm), _blk(n, bn)
    a_spec = pl.BlockSpec((bm, kg), lambda i, j: (i, 0))
    return pl.pallas_call(
        _mm_groups_res_kernel,
        grid=(m // bm, n // bn),
        in_specs=[a_spec, a_spec, a_spec, a_spec,
                  pl.BlockSpec((None, k, bn), lambda i, j: (layer, 0, j)),
                  pl.BlockSpec((bm, bn), lambda i, j: (i, j))],
        out_specs=pl.BlockSpec((bm, bn), lambda i, j: (i, j)),
        out_shape=jax.ShapeDtypeStruct((m, n), F32),
        compiler_params=_params("parallel", "parallel"),
        name="matmul_groups_residual",
    )(*groups, w, res)


def _conv_kernel(h_ref, w_ref, b_ref, g_ref, beta_ref, o_ref, ybuf, cbuf, *, ts, c):
    s = pl.program_id(1)

    @pl.when(s == 0)
    def _():
        ybuf[0:CONV_HALO, :] = jnp.zeros((CONV_HALO, c), F32)
        ybuf[CONV_HALO + ts:CONV_HALO + ts + SUBLANES, :] = jnp.zeros((SUBLANES, c), F32)

    @pl.when(s > 0)
    def _():
        ybuf[0:CONV_HALO, :] = ybuf[ts:ts + CONV_HALO, :]

    a = h_ref[0, :, 0:c]
    gate = h_ref[0, :, c:2 * c]
    ybuf[CONV_HALO:CONV_HALO + ts, :] = a * jax.nn.sigmoid(gate)

    row0 = CONV_HALO - (CONV_K - 1)
    span = ts + SUBLANES
    for cb in range(c // LANES):
        lanes = slice(cb * LANES, (cb + 1) * LANES)
        out = jnp.broadcast_to(b_ref[:, lanes], (ts, LANES))
        for p in range(SUBLANES):
            part = None
            for j in range(CONV_K):
                if (row0 + j) % SUBLANES == p:
                    r0 = row0 + j - p
                    term = w_ref[j:j + 1, lanes] * ybuf[r0:r0 + span, lanes]
                    part = term if part is None else part + term
            out = out + part[p:p + ts, :]
        cbuf[:, lanes] = out

    y = cbuf[...]
    mu = jnp.mean(y, axis=-1, keepdims=True)
    d = y - mu
    var = jnp.mean(d * d, axis=-1, keepdims=True)
    y = d * lax.rsqrt(var + EPS) * g_ref[...] + beta_ref[...]
    o_ref[0] = jax.nn.silu(y).astype(o_ref.dtype)


def _conv_mixer(h, w_dw, b_dw, ln_g, ln_b, ts=128):
    b, s, c2 = h.shape
    c = c2 // 2
    vec = pl.BlockSpec((1, c), lambda bi, si: (0, 0))
    return pl.pallas_call(
        functools.partial(_conv_kernel, ts=ts, c=c),
        grid=(b, s // ts),
        in_specs=[pl.BlockSpec((1, ts, c2), lambda bi, si: (bi, si, 0)),
                  pl.BlockSpec((CONV_K, c), lambda bi, si: (0, 0)),
                  vec, vec, vec],
        out_specs=pl.BlockSpec((1, ts, c), lambda bi, si: (bi, si, 0)),
        out_shape=jax.ShapeDtypeStruct((b, s, c), BF16),
        scratch_shapes=[pltpu.VMEM((ts + CONV_HALO + SUBLANES, c), F32), pltpu.VMEM((ts, c), F32)],
        compiler_params=_params("parallel", "arbitrary"),
        name="conv_mixer",
    )(h, w_dw, b_dw.reshape(1, c), ln_g.reshape(1, c), ln_b.reshape(1, c))


def _sgu_kernel(h_ref, g_ref, beta_ref, ws_ref, bs_ref, o_ref, *, ts, c):
    hh = jax.nn.gelu(h_ref[0])
    u = hh[:, 0:c]
    v = hh[:, c:2 * c]
    mu = jnp.mean(v, axis=-1, keepdims=True)
    d = v - mu
    var = jnp.mean(d * d, axis=-1, keepdims=True)
    v = (d * lax.rsqrt(var + EPS) * g_ref[...] + beta_ref[...]).astype(BF16)
    row = lax.broadcasted_iota(jnp.int32, (SGU_BLOCK, SGU_BLOCK), 0)
    col = lax.broadcasted_iota(jnp.int32, (SGU_BLOCK, SGU_BLOCK), 1)
    chunk_causal = (col // CHUNK) <= (row // CHUNK)
    for hd in range(c // SGU_HEAD):
        lanes = slice(hd * SGU_HEAD, (hd + 1) * SGU_HEAD)
        w = jnp.where(chunk_causal, ws_ref[hd], 0.0).astype(BF16)
        for blk in range(ts // SGU_BLOCK):
            rows = slice(blk * SGU_BLOCK, (blk + 1) * SGU_BLOCK)
            sv = _dot(w, v[rows, lanes]) + bs_ref[:, lanes]
            o_ref[0, rows, lanes] = (u[rows, lanes] * sv).astype(o_ref.dtype)


def _sgu_mixer(h, ln_g, ln_b, w_s, b_s, ts=256):
    b, s, c2 = h.shape
    c = c2 // 2
    heads = w_s.shape[0]
    bias = jnp.repeat(b_s.T, c // heads, axis=1)
    vec = pl.BlockSpec((1, c), lambda bi, si: (0, 0))
    return pl.pallas_call(
        functools.partial(_sgu_kernel, ts=ts, c=c),
        grid=(b, s // ts),
        in_specs=[pl.BlockSpec((1, ts, c2), lambda bi, si: (bi, si, 0)),
                  vec, vec,
                  pl.BlockSpec((heads, SGU_BLOCK, SGU_BLOCK), lambda bi, si: (0, 0, 0)),
                  pl.BlockSpec((SGU_BLOCK, c), lambda bi, si: (0, 0))],
        out_specs=pl.BlockSpec((1, ts, c), lambda bi, si: (bi, si, 0)),
        out_shape=jax.ShapeDtypeStruct((b, s, c), BF16),
        compiler_params=_params("parallel", "parallel"),
        name="sgu_mixer",
    )(h, ln_g.reshape(1, c), ln_b.reshape(1, c), w_s, bias)


def _softplus(z):
    return jnp.maximum(z, 0.0) + jnp.log1p(jnp.exp(-jnp.abs(z)))


def _sb_kernel(q_ref, k_ref, v_ref, o_ref, *, s, scale, heads):
    blk = SB_BLOCK
    row = lax.broadcasted_iota(jnp.int32, (blk, blk), 0)
    col = lax.broadcasted_iota(jnp.int32, (blk, blk), 1)
    before = col < row
    jrow = lax.broadcasted_iota(jnp.int32, (2 * blk, 2 * blk), 0) % blk
    scol = lax.broadcasted_iota(jnp.int32, (2 * blk, 2 * blk), 1)
    u2 = jnp.where((jrow > scol) | (scol >= blk), 1.0, 0.0).astype(BF16)

    def suffix(lk):
        hi = lk.astype(BF16)
        lo = (lk - hi.astype(F32)).astype(BF16)
        r = _dot(jnp.concatenate([hi, lo], axis=1), u2)
        return r[:, 0:blk], r[:, blk:2 * blk]

    head_lanes = [slice(hd * SB_HEAD, (hd + 1) * SB_HEAD) for hd in range(heads)]

    def qblock(qi, _):
        q0 = pl.multiple_of(qi * blk, blk)
        qbs = [q_ref[0, pl.ds(q0, blk), lanes] for lanes in head_lanes]

        def key_blocks(j0s, st, diag):
            z = [[lax.dot_general(qb, k_ref[0, pl.ds(j0, blk), lanes], _NT, preferred_element_type=F32) * scale
                  for qb, lanes in zip(qbs, head_lanes)] for j0 in j0s]
            sp = [[_softplus(x) for x in zs] for zs in z]
            lk = [[-x for x in sps] for sps in sp]
            if diag:
                lk = [[jnp.where(before, x, 0.0) for x in lks] for lks in lk]
            suf = [[suffix(x) for x in lks] for lks in lk]
            carry = [c for c, _ in st]
            acc = [a for _, a in st]
            for j0, zs, sps, sufs in zip(j0s, z, sp, suf):
                wts = [jnp.exp((zz - ss) + (later + c)) for zz, ss, (later, _), c in zip(zs, sps, sufs, carry)]
                if diag:
                    wts = [jnp.where(before, x, 0.0) for x in wts]
                acc = [a + _dot(x.astype(BF16), v_ref[0, pl.ds(j0, blk), lanes])
                       for a, x, lanes in zip(acc, wts, head_lanes)]
                carry = [c + total for c, (_, total) in zip(carry, sufs)]
            return tuple(zip(carry, acc))

        zeros = jnp.zeros((blk, blk), F32)
        st = key_blocks([q0], ((zeros, zeros),) * heads, True)

        def kstep2(it, st):
            ja = pl.multiple_of((qi - 1 - 2 * it) * blk, blk)
            return key_blocks([ja, pl.multiple_of(ja - blk, blk)], st, False)

        st = lax.fori_loop(0, qi // 2, kstep2, st)
        st = lax.fori_loop(0, qi % 2, lambda it, st: key_blocks([0], st, False), st)
        for hd, lanes in enumerate(head_lanes):
            o_ref[0, pl.ds(q0, blk), lanes] = st[hd][1].astype(o_ref.dtype)
        return 0

    lax.fori_loop(0, s // blk, qblock, 0)


def _sb_mixer(h, heads_per_step=8):
    b, s, c3 = h.shape
    c = c3 // 3
    w = heads_per_step * SB_HEAD
    groups = c // w
    return pl.pallas_call(
        functools.partial(_sb_kernel, s=s, scale=SB_HEAD ** -0.5, heads=heads_per_step),
        grid=(b, groups),
        in_specs=[pl.BlockSpec((1, s, w), lambda bi, gi: (bi, 0, gi)),
                  pl.BlockSpec((1, s, w), lambda bi, gi: (bi, 0, gi + groups)),
                  pl.BlockSpec((1, s, w), lambda bi, gi: (bi, 0, gi + 2 * groups))],
        out_specs=pl.BlockSpec((1, s, w), lambda bi, gi: (bi, 0, gi)),
        out_shape=jax.ShapeDtypeStruct((b, s, c), BF16),
        compiler_params=_params("parallel", "parallel"),
        name="sb_mixer",
    )(h, h, h)


def _rwkv_prep_kernel(h_ref, l_ref, mu_ref, mul_ref, w0_ref, wup_ref, a0_ref, aup_ref, gup_ref, kk_ref, ka_ref,
                      r_ref, lw_ref, kkr_ref, a_ref, kf_ref, v_ref, g_ref, hbuf, lbuf, *, ts, c):
    s = pl.program_id(1)
    c3 = 3 * c
    nl = l_ref.shape[2]

    @pl.when(s == 0)
    def _():
        hbuf[0:SUBLANES, :] = jnp.zeros((SUBLANES, c3), F32)
        lbuf[0:SUBLANES, :] = jnp.zeros((SUBLANES, nl), F32)

    @pl.when(s > 0)
    def _():
        hbuf[0:SUBLANES, :] = hbuf[ts:ts + SUBLANES, :]
        lbuf[0:SUBLANES, :] = lbuf[ts:ts + SUBLANES, :]

    h = h_ref[0]
    lo = l_ref[0]
    hbuf[SUBLANES:SUBLANES + ts, :] = h
    lbuf[SUBLANES:SUBLANES + ts, :] = lo
    h = h + (hbuf[SUBLANES - 1:SUBLANES - 1 + ts, :] - h) * mu_ref[...]
    lo = lo + (lbuf[SUBLANES - 1:SUBLANES - 1 + ts, :] - lo) * mul_ref[...]

    r = h[:, 0:c]
    k = h[:, c:2 * c]
    xw = lo[:, 0:LANES]
    xa = lo[:, LANES:2 * LANES]
    xg = lo[:, 2 * LANES:nl]

    w_log = -_softplus(-(w0_ref[...] + _dot(jnp.tanh(xw).astype(BF16), wup_ref[...]))) - 0.5
    a = jax.nn.sigmoid(a0_ref[...] + _dot(xa.astype(BF16), aup_ref[...]))
    r_ref[0] = r
    lw_ref[0] = -jnp.exp(w_log)
    kkr_ref[0] = k * kk_ref[...]
    a_ref[0] = a
    kf_ref[0] = k * (1.0 + (a - 1.0) * ka_ref[...])
    v_ref[0] = h[:, 2 * c:3 * c]
    g_ref[0] = _dot(jax.nn.sigmoid(xg).astype(BF16), gup_ref[...])


def _rwkv_prep(h, lora, mu, mul, w0, w_up, a0, a_up, g_up, k_k, k_a, ts=256):
    b, s, c3 = h.shape
    c = c3 // 3
    nl = lora.shape[2]
    nat = pl.BlockSpec((1, ts, c), lambda bi, si: (bi, si, 0))
    vec = lambda n: pl.BlockSpec((1, n), lambda bi, si: (0, 0))
    full = lambda a: pl.BlockSpec(a.shape, lambda bi, si: (0, 0))
    return pl.pallas_call(
        functools.partial(_rwkv_prep_kernel, ts=ts, c=c),
        grid=(b, s // ts),
        in_specs=[pl.BlockSpec((1, ts, c3), lambda bi, si: (bi, si, 0)),
                  pl.BlockSpec((1, ts, nl), lambda bi, si: (bi, si, 0)),
                  vec(c3), vec(nl), vec(c), full(w_up), vec(c), full(a_up), full(g_up), vec(c), vec(c)],
        out_specs=[nat] * 7,
        out_shape=[jax.ShapeDtypeStruct((b, s, c), F32)] * 7,
        scratch_shapes=[pltpu.VMEM((ts + SUBLANES, c3), F32), pltpu.VMEM((ts + SUBLANES, nl), F32)],
        compiler_params=_params("parallel", "arbitrary"),
        name="rwkv_prep",
    )(h, lora, mu.reshape(1, c3), mul.reshape(1, nl), w0.reshape(1, c), w_up, a0.reshape(1, c), a_up, g_up,
      k_k.reshape(1, c), k_a.reshape(1, c))


def _split(x):
    hi = x.astype(BF16)
    return hi, (x - hi.astype(F32)).astype(BF16)


def _dot3(a, b, dims=_NN):
    lhs = jnp.concatenate([a[0], a[1]], axis=1)
    zero = jnp.zeros_like(b[0])
    if dims is _NN:
        rhs = jnp.concatenate([jnp.concatenate([b[0], b[1]], axis=1),
                               jnp.concatenate([b[0], zero], axis=1)], axis=0)
    else:
        rhs = jnp.concatenate([jnp.concatenate([b[0], b[0]], axis=1),
                               jnp.concatenate([b[1], zero], axis=1)], axis=0)
    out = lax.dot_general(lhs, rhs, dims, preferred_element_type=F32)
    n = out.shape[1] // 2
    return out[:, :n] + out[:, n:]


def _rwkv_chunk_kernel(r_ref, lw_ref, kkr_ref, a_ref, kf_ref, v_ref, g_ref, rk_ref, lg_ref, lb_ref,
                       o_ref, state, *, tt, groups):
    n, t = RWKV_HEAD, RWKV_CHUNK
    t2 = 2 * t

    @pl.when(pl.program_id(2) == 0)
    def _():
        state[...] = jnp.zeros_like(state)

    first = lax.broadcasted_iota(jnp.int32, (t, LANES), 1) < n
    row = lax.broadcasted_iota(jnp.int32, (t2, t2), 0)
    col = lax.broadcasted_iota(jnp.int32, (t2, t2), 1)
    strict, incl = row > col, row >= col
    eye = jnp.where(row == col, 1.0, 0.0)
    tri3 = jnp.where(lax.broadcasted_iota(jnp.int32, (t, 3 * t), 0) >= lax.broadcasted_iota(jnp.int32, (t, 3 * t), 1) % t,
                     1.0, 0.0).astype(BF16)

    def head_sum(x):
        s0 = jnp.sum(jnp.where(first, x, 0.0), axis=1, keepdims=True)
        s1 = jnp.sum(jnp.where(first, 0.0, x), axis=1, keepdims=True)
        return jnp.where(first, s0, s1)

    def stack(x):
        return jnp.concatenate([jnp.where(first, x, 0.0), jnp.where(first, 0.0, x)], axis=0)

    cat = lambda *xs: jnp.concatenate(xs, axis=0)
    nch = tt // t
    units = [(slice(c * t, (c + 1) * t), slice(g * LANES, (g + 1) * LANES)) for g in range(groups) for c in range(nch)]
    each = lambda fn, *lists: [fn(*xs) for xs in zip(*lists)]

    def setup(rw, ln):
        r, lw, kkr, a, kf, v = (ref[0, rw, ln] for ref in (r_ref, lw_ref, kkr_ref, a_ref, kf_ref, v_ref))
        h1 = lw.astype(BF16)
        r1 = lw - h1.astype(F32)
        h2 = r1.astype(BF16)
        h3 = (r1 - h2.astype(F32)).astype(BF16)
        cum = _dot(tri3, cat(h1, h2, h3))
        cum_t = cum[t - 1:t, :]
        kk = kkr / jnp.maximum(jnp.sqrt(head_sum(kkr * kkr)), 1e-12)
        b = kk * a
        inv_p = jnp.exp(-cum)
        to_end = jnp.exp(cum_t - cum)
        ks = stack(kk * jnp.exp(cum - lw))
        kr = _split(cat(ks, stack(r * jnp.exp(cum))))
        bp = _split(cat(stack(b * inv_p), stack(kf * inv_p)))
        ends = _split(cat(stack(b * to_end), stack(kf * to_end)))
        bonus = head_sum(r * kf * rk_ref[:, ln]) * v
        return ks, kr, bp, ends, stack(v), jnp.exp(cum_t), bonus

    ks, kr, bp, ends, vs, decay_t, bonus = zip(*[setup(rw, ln) for rw, ln in units])
    gram = each(lambda x, y: _dot3(x, y, _NT), kr, bp)
    a_m = [jnp.where(strict, g[:t2, :t2], 0.0) for g in gram]
    catp = lambda x, y: (cat(x[0], y[0]), cat(x[1], y[1]))
    minv = [eye - x for x in a_m]
    pw = [_split(x) for x in a_m]
    pw = [_split(_dot3(x, x)) for x in pw]
    for _ in range(t.bit_length() - 3):
        both = each(lambda p, m: _dot3(catp(p, _split(m)), p), pw, minv)
        minv = each(lambda m, bo: m + bo[t2:], minv, both)
        pw = [_split(bo[:t2]) for bo in both]
    minv = each(lambda m, p: m + _dot3(_split(m), p), minv, pw)
    ms = [_split(m) for m in minv]
    vss = [_split(x) for x in vs]
    lv = each(lambda g, x: _dot3(_split(cat(jnp.where(strict, g[:t2, t2:], 0.0),
                                            jnp.where(incl, g[t2:, t2:], 0.0))), x), gram, vss)
    ds = [x[t2:] for x in lv]
    ktcs = each(lambda m, x, y: _dot3(m, _split(jnp.concatenate([x, y[:t2]], axis=1))), ms, ks, lv)
    kt = [_split(x[:, :LANES]) for x in ktcs]
    cs = [x[:, LANES:] for x in ktcs]
    lbs = [_split(jnp.where(incl, g[t2:, :t2], 0.0)) for g in gram]
    lhs = each(lambda k, x: (cat(k[0], x[0][t2:]), cat(k[1], x[1][t2:])), kt, kr)

    st = [state[g] for g in range(groups)]
    for c in range(nch):
        ids = [g * nch + c for g in range(groups)]
        x = [_dot3(lhs[i], _split(s_g), _NT) for i, s_g in zip(ids, st)]
        us = [-xx[:t2] - cs[i] for i, xx in zip(ids, x)]
        ys = [xx[t2:] + _dot3(lbs[i], _split(u)) + ds[i] for i, xx, u in zip(ids, x, us)]
        uv_t = [cat(u, vs[i]).T for i, u in zip(ids, us)]
        st = [s_g * decay_t[i] + _dot3(_split(m), ends[i]) for i, s_g, m in zip(ids, st, uv_t)]
        for i, yy in zip(ids, ys):
            rw, ln = units[i]
            y = yy[:t] + yy[t:]
            mu = head_sum(y) * (1.0 / n)
            d = y - mu
            var = head_sum(d * d) * (1.0 / n)
            y = d * lax.rsqrt(var + LNX_EPS) * lg_ref[:, ln] + lb_ref[:, ln]
            o_ref[0, rw, ln] = ((y + bonus[i]) * g_ref[0, rw, ln]).astype(o_ref.dtype)
    for g in range(groups):
        state[g] = st[g]


def _rwkv_chunks(r, lw, kkr, a, kf, v, g, r_k, lnx_g, lnx_b, tt=128, groups=4):
    b, s, c = v.shape
    tt = _blk(s, tt)
    cw = groups * LANES
    nat = pl.BlockSpec((1, tt, cw), lambda bi, pi, si: (bi, si, pi))
    vec = pl.BlockSpec((1, cw), lambda bi, pi, si: (0, pi))
    return pl.pallas_call(
        functools.partial(_rwkv_chunk_kernel, tt=tt, groups=groups),
        grid=(b, c // cw, s // tt),
        in_specs=[nat] * 7 + [vec] * 3,
        out_specs=nat,
        out_shape=jax.ShapeDtypeStruct((b, s, c), BF16),
        scratch_shapes=[pltpu.VMEM((groups, LANES, LANES), F32)],
        compiler_params=_params("parallel", "parallel", "arbitrary"),
        name="rwkv_chunks",
    )(r, lw, kkr, a, kf, v, g, r_k.reshape(1, c), lnx_g.reshape(1, c), lnx_b.reshape(1, c))


def _pad_rows(w, rows):
    return jnp.pad(w, ((0, rows - w.shape[0]), (0, 0)))


def _rwkv_mixer(h, lora, mu, w0, w_up, a0, a_up, g_up, k_k, k_a, r_k, lnx_g, lnx_b):
    c = h.shape[2] // 3
    mu_main = mu[:3 * c]
    mu_l = _lora_slots(mu[None, 3 * c:])[0]
    r, lw, kkr, a, kf, v, g = _rwkv_prep(
        h, lora, mu_main, mu_l, w0, _pad_rows(w_up, LANES).astype(BF16), a0,
        _pad_rows(a_up, LANES).astype(BF16), _pad_rows(g_up, 2 * LANES).astype(BF16), k_k, k_a)
    return _rwkv_chunks(r, lw, kkr, a, kf, v, g, r_k.reshape(-1), lnx_g, lnx_b)


def _lora_slots(w):
    xw = w[:, :DECAY_LORA]
    xa = w[:, DECAY_LORA:DECAY_LORA + ICLR_LORA]
    xg = w[:, DECAY_LORA + ICLR_LORA:]
    padto = lambda t, n: jnp.pad(t, ((0, 0), (0, n - t.shape[1])))
    return jnp.concatenate([padto(xw, LANES), padto(xa, LANES), padto(xg, 2 * LANES)], axis=1)


def _ffn(x2d, norm_g, w_in, w_out, layer):
    xn = _rmsnorm(x2d, norm_g, BF16)
    hid, w_out_bf16 = _matmul_swiglu(xn, w_in, w_out, layer, BF16)
    return _matmul_residual(hid, w_out_bf16, x2d, 0.5)


def kernel(x, ffn1_norm, ffn1_w_in, ffn1_w_out, mix_norm, mix_w_in, conv_w, conv_b, conv_ln_g, conv_ln_b, rwkv_mu, rwkv_w0, rwkv_w_up, rwkv_a0, rwkv_a_up, rwkv_g_up, rwkv_k_k, rwkv_k_a, rwkv_r_k, rwkv_lnx_g, rwkv_lnx_b, sgu_ln_g, sgu_ln_b, sgu_w_s, sgu_b_s, mix_w_out, ffn2_norm, ffn2_w_in, ffn2_w_out, final_norm):
    b, s, d = x.shape
    depth = ffn1_norm.shape[0]
    c = conv_w.shape[2]
    o_lora = 5 * c
    o_sgu = o_lora + DECAY_LORA + ICLR_LORA + GATE_LORA
    w_mix_t = jnp.swapaxes(mix_w_in, 1, 2)

    x2d = x.reshape(b * s, d)
    for l in range(depth):
        x2d = _ffn(x2d, ffn1_norm[l], ffn1_w_in, ffn1_w_out, l)

        xn = _rmsnorm(x2d, mix_norm[l], BF16)
        mm = lambda wt, lyr, row0, n, dt: _matmul_nt(xn, wt, lyr, row0, n, dt).reshape(b, s, n)
        w_lora_t = _lora_slots(mix_w_in[l, :, o_lora:o_sgu]).T[None]
        h_conv = mm(w_mix_t, l, 0, 2 * c, F32)
        h_rkv = mm(w_mix_t, l, 2 * c, 3 * c, F32)
        h_lora = mm(w_lora_t, 0, 0, w_lora_t.shape[1], F32)
        h_sgu = mm(w_mix_t, l, o_sgu, 2 * c, F32)
        h_sb = mm(w_mix_t, l, o_sgu + 2 * c, 3 * c, BF16)

        y_conv = _conv_mixer(h_conv, conv_w[l], conv_b[l], conv_ln_g[l], conv_ln_b[l])
        y_rwkv = _rwkv_mixer(h_rkv, h_lora, rwkv_mu[l], rwkv_w0[l], rwkv_w_up[l], rwkv_a0[l], rwkv_a_up[l],
                             rwkv_g_up[l], rwkv_k_k[l], rwkv_k_a[l], rwkv_r_k[l], rwkv_lnx_g[l], rwkv_lnx_b[l])
        y_sgu = _sgu_mixer(h_sgu, sgu_ln_g[l], sgu_ln_b[l], sgu_w_s[l], sgu_b_s[l])
        y_sb = _sb_mixer(h_sb)
        groups = [t.reshape(b * s, c) for t in (y_conv, y_rwkv, y_sgu, y_sb)]
        x2d = _matmul_groups_residual(groups, mix_w_out, l, x2d)

        x2d = _ffn(x2d, ffn2_norm[l], ffn2_w_in, ffn2_w_out, l)
    return _rmsnorm(x2d, final_norm, F32).reshape(b, s, d)
```

```python
import functools

import jax
import jax.numpy as jnp
from jax import lax
from jax.experimental import pallas as pl
from jax.experimental.pallas import tpu as pltpu

LANES = 128
SUBLANES = 8
VMEM_LIMIT_BYTES = 52 * 1024 * 1024
VMEM_LIMIT_WIDE_BYTES = 57 * 1024 * 1024

CHUNK = 64
CONV_K = 31
CONV_HALO = 32
RWKV_HEAD = 64
RWKV_CHUNK = 64
DECAY_LORA = 64
ICLR_LORA = 64
GATE_LORA = 160
LNX_EPS = 64e-5
SGU_BLOCK = 128
SGU_HEAD = 128
SB_HEAD = 128
SB_BLOCK = 128
SB_KEYS_PER_STEP = 4
EPS = 1e-6

BF16 = jnp.bfloat16
F32 = jnp.float32


def _params(*sem, vmem=VMEM_LIMIT_BYTES):
    return pltpu.CompilerParams(dimension_semantics=sem, vmem_limit_bytes=vmem)


_NN = (((1,), (0,)), ((), ()))
_NT = (((1,), (1,)), ((), ()))


def _dot(a, b):
    return jnp.dot(a, b, preferred_element_type=F32)


def _blk(n, pref):
    if n <= pref:
        return n
    for cand in range(pref - pref % LANES, 0, -LANES):
        if n % cand == 0:
            return cand
    return n


def _rmsnorm_kernel(x_ref, g_ref, o_ref):
    x = x_ref[...]
    ms = jnp.mean(x * x, axis=-1, keepdims=True)
    o_ref[...] = (x * lax.rsqrt(ms + EPS) * g_ref[...]).astype(o_ref.dtype)


def _rmsnorm(x, g, out_dtype, bm=256):
    m, d = x.shape
    bm = _blk(m, bm)
    return pl.pallas_call(
        _rmsnorm_kernel,
        grid=(m // bm,),
        in_specs=[pl.BlockSpec((bm, d), lambda i: (i, 0)),
                  pl.BlockSpec((1, d), lambda i: (0, 0))],
        out_specs=pl.BlockSpec((bm, d), lambda i: (i, 0)),
        out_shape=jax.ShapeDtypeStruct((m, d), out_dtype),
        compiler_params=_params("parallel"),
        name="rmsnorm",
    )(x, g.reshape(1, d))


def _mm_nt_kernel(a_ref, w_ref, o_ref):
    o_ref[...] = lax.dot_general(a_ref[...], w_ref[0].astype(BF16), _NT,
                                 preferred_element_type=F32).astype(o_ref.dtype)


def _matmul_nt(a, wt, layer, row0, n, out_dtype, bm=2048, bn=256):
    m, k = a.shape
    bm, bn = _blk(m, bm), _blk(n, bn)
    assert row0 % SUBLANES == 0
    w_spec = pl.BlockSpec((pl.Element(1), pl.Element(bn), pl.Element(k)),
                          lambda i, j: (layer, pl.multiple_of(row0 + j * bn, SUBLANES), 0))
    return pl.pallas_call(
        _mm_nt_kernel,
        grid=(m // bm, n // bn),
        in_specs=[pl.BlockSpec((bm, k), lambda i, j: (i, 0)), w_spec],
        out_specs=pl.BlockSpec((bm, bn), lambda i, j: (i, j)),
        out_shape=jax.ShapeDtypeStruct((m, n), out_dtype),
        compiler_params=_params("parallel", "parallel", vmem=VMEM_LIMIT_WIDE_BYTES),
        name="matmul_nt",
    )(a, wt)


def _mm_swiglu_kernel(a_ref, wg_ref, wu_ref, wo_ref, o_ref, wo_bf16_ref):
    a = a_ref[...]
    gate = _dot(a, wg_ref[...].astype(BF16))
    up = _dot(a, wu_ref[...].astype(BF16))
    o_ref[...] = (jax.nn.silu(gate) * up).astype(o_ref.dtype)
    wo_bf16_ref[...] = wo_ref[...].astype(BF16)


def _matmul_swiglu(a, w, w_out, layer, out_dtype, bm=2048, bn=256):
    m, k = a.shape
    f = w.shape[2] // 2
    bm, bn = _blk(m, bm), _blk(f, bn)
    nj = f // bn
    steps = (m // bm) * nj
    slab = w_out.shape[1] // steps
    assert slab * steps == w_out.shape[1] and slab % (2 * SUBLANES) == 0
    n = w_out.shape[2]
    return pl.pallas_call(
        _mm_swiglu_kernel,
        grid=(m // bm, nj),
        in_specs=[pl.BlockSpec((bm, k), lambda i, j: (i, 0)),
                  pl.BlockSpec((None, k, bn), lambda i, j: (layer, 0, j)),
                  pl.BlockSpec((None, k, bn), lambda i, j: (layer, 0, j + nj)),
                  pl.BlockSpec((None, slab, n), lambda i, j: (layer, i * nj + j, 0))],
        out_specs=[pl.BlockSpec((bm, bn), lambda i, j: (i, j)),
                   pl.BlockSpec((slab, n), lambda i, j: (i * nj + j, 0))],
        out_shape=[jax.ShapeDtypeStruct((m, f), out_dtype),
                   jax.ShapeDtypeStruct((w_out.shape[1], n), BF16)],
        compiler_params=_params("parallel", "arbitrary", vmem=VMEM_LIMIT_WIDE_BYTES),
        name="matmul_swiglu",
    )(a, w, w, w_out)


def _mm_res_kernel(a_ref, w_ref, r_ref, o_ref, *, scale):
    o_ref[...] = r_ref[...] + scale * _dot(a_ref[...], w_ref[...])


def _matmul_residual(a, w, res, scale, bm=512, bn=512):
    m, k = a.shape
    n = w.shape[1]
    bm, bn = _blk(m, bm), _blk(n, bn)
    return pl.pallas_call(
        functools.partial(_mm_res_kernel, scale=scale),
        grid=(m // bm, n // bn),
        in_specs=[pl.BlockSpec((bm, k), lambda i, j: (i, 0)),
                  pl.BlockSpec((k, bn), lambda i, j: (0, j)),
                  pl.BlockSpec((bm, bn), lambda i, j: (i, j))],
        out_specs=pl.BlockSpec((bm, bn), lambda i, j: (i, j)),
        out_shape=jax.ShapeDtypeStruct((m, n), F32),
        compiler_params=_params("parallel", "parallel"),
        name="matmul_residual",
    )(a, w, res)


def _mm_groups_res_kernel(a0_ref, a1_ref, a2_ref, a3_ref, w_ref, r_ref, o_ref):
    kg = a0_ref.shape[1]
    acc = r_ref[...]
    for g, a_ref in enumerate((a0_ref, a1_ref, a2_ref, a3_ref)):
        acc = acc + _dot(a_ref[...], w_ref[g * kg:(g + 1) * kg, :].astype(BF16))
    o_ref[...] = acc


def _matmul_groups_residual(groups, w, layer, res, bm=1024, bn=512):
    m, kg = groups[0].shape
    _, k, n = w.shape
    bm, bn = _blk(m, bm), _blk(n, bn)
    a_spec = pl.BlockSpec((bm, kg), lambda i, j: (i, 0))
    return pl.pallas_call(
        _mm_groups_res_kernel,
        grid=(m // bm, n // bn),
        in_specs=[a_spec, a_spec, a_spec, a_spec,
                  pl.BlockSpec((None, k, bn), lambda i, j: (layer, 0, j)),
                  pl.BlockSpec((bm, bn), lambda i, j: (i, j))],
        out_specs=pl.BlockSpec((bm, bn), lambda i, j: (i, j)),
        out_shape=jax.ShapeDtypeStruct((m, n), F32),
        compiler_params=_params("parallel", "parallel"),
        name="matmul_groups_residual",
    )(*groups, w, res)


def _conv_kernel(h_ref, w_ref, b_ref, g_ref, beta_ref, o_ref, ybuf, cbuf, *, ts, c):
    s = pl.program_id(1)

    @pl.when(s == 0)
    def _():
        ybuf[0:CONV_HALO, :] = jnp.zeros((CONV_HALO, c), F32)
        ybuf[CONV_HALO + ts:CONV_HALO + ts + SUBLANES, :] = jnp.zeros((SUBLANES, c), F32)

    @pl.when(s > 0)
    def _():
        ybuf[0:CONV_HALO, :] = ybuf[ts:ts + CONV_HALO, :]

    a = h_ref[0, :, 0:c]
    gate = h_ref[0, :, c:2 * c]
    ybuf[CONV_HALO:CONV_HALO + ts, :] = a * jax.nn.sigmoid(gate)

    row0 = CONV_HALO - (CONV_K - 1)
    span = ts + SUBLANES
    for cb in range(c // LANES):
        lanes = slice(cb * LANES, (cb + 1) * LANES)
        out = jnp.broadcast_to(b_ref[:, lanes], (ts, LANES))
        for p in range(SUBLANES):
            part = None
            for j in range(CONV_K):
                if (row0 + j) % SUBLANES == p:
                    r0 = row0 + j - p
                    term = w_ref[j:j + 1, lanes] * ybuf[r0:r0 + span, lanes]
                    part = term if part is None else part + term
            out = out + part[p:p + ts, :]
        cbuf[:, lanes] = out

    y = cbuf[...]
    mu = jnp.mean(y, axis=-1, keepdims=True)
    d = y - mu
    var = jnp.mean(d * d, axis=-1, keepdims=True)
    y = d * lax.rsqrt(var + EPS) * g_ref[...] + beta_ref[...]
    o_ref[0] = jax.nn.silu(y).astype(o_ref.dtype)


def _conv_mixer(h, w_dw, b_dw, ln_g, ln_b, ts=128):
    b, s, c2 = h.shape
    c = c2 // 2
    vec = pl.BlockSpec((1, c), lambda bi, si: (0, 0))
    return pl.pallas_call(
        functools.partial(_conv_kernel, ts=ts, c=c),
        grid=(b, s // ts),
        in_specs=[pl.BlockSpec((1, ts, c2), lambda bi, si: (bi, si, 0)),
                  pl.BlockSpec((CONV_K, c), lambda bi, si: (0, 0)),
                  vec, vec, vec],
        out_specs=pl.BlockSpec((1, ts, c), lambda bi, si: (bi, si, 0)),
        out_shape=jax.ShapeDtypeStruct((b, s, c), BF16),
        scratch_shapes=[pltpu.VMEM((ts + CONV_HALO + SUBLANES, c), F32), pltpu.VMEM((ts, c), F32)],
        compiler_params=_params("parallel", "arbitrary"),
        name="conv_mixer",
    )(h, w_dw, b_dw.reshape(1, c), ln_g.reshape(1, c), ln_b.reshape(1, c))


def _sgu_kernel(h_ref, g_ref, beta_ref, ws_ref, bs_ref, o_ref, *, ts, c):
    hh = jax.nn.gelu(h_ref[0])
    u = hh[:, 0:c]
    v = hh[:, c:2 * c]
    mu = jnp.mean(v, axis=-1, keepdims=True)
    d = v - mu
    var = jnp.mean(d * d, axis=-1, keepdims=True)
    v = (d * lax.rsqrt(var + EPS) * g_ref[...] + beta_ref[...]).astype(BF16)
    row = lax.broadcasted_iota(jnp.int32, (SGU_BLOCK, SGU_BLOCK), 0)
    col = lax.broadcasted_iota(jnp.int32, (SGU_BLOCK, SGU_BLOCK), 1)
    chunk_causal = (col // CHUNK) <= (row // CHUNK)
    for hd in range(c // SGU_HEAD):
        lanes = slice(hd * SGU_HEAD, (hd + 1) * SGU_HEAD)
        w = jnp.where(chunk_causal, ws_ref[hd], 0.0).astype(BF16)
        for blk in range(ts // SGU_BLOCK):
            rows = slice(blk * SGU_BLOCK, (blk + 1) * SGU_BLOCK)
            sv = _dot(w, v[rows, lanes]) + bs_ref[:, lanes]
            o_ref[0, rows, lanes] = (u[rows, lanes] * sv).astype(o_ref.dtype)


def _sgu_mixer(h, ln_g, ln_b, w_s, b_s, ts=256):
    b, s, c2 = h.shape
    c = c2 // 2
    heads = w_s.shape[0]
    bias = jnp.repeat(b_s.T, c // heads, axis=1)
    vec = pl.BlockSpec((1, c), lambda bi, si: (0, 0))
    return pl.pallas_call(
        functools.partial(_sgu_kernel, ts=ts, c=c),
        grid=(b, s // ts),
        in_specs=[pl.BlockSpec((1, ts, c2), lambda bi, si: (bi, si, 0)),
                  vec, vec,
                  pl.BlockSpec((heads, SGU_BLOCK, SGU_BLOCK), lambda bi, si: (0, 0, 0)),
                  pl.BlockSpec((SGU_BLOCK, c), lambda bi, si: (0, 0))],
        out_specs=pl.BlockSpec((1, ts, c), lambda bi, si: (bi, si, 0)),
        out_shape=jax.ShapeDtypeStruct((b, s, c), BF16),
        compiler_params=_params("parallel", "parallel"),
        name="sgu_mixer",
    )(h, ln_g.reshape(1, c), ln_b.reshape(1, c), w_s, bias)


def _softplus(z):
    return jnp.maximum(z, 0.0) + jnp.log1p(jnp.exp(-jnp.abs(z)))


def _sb_kernel(q_ref, k_ref, v_ref, o_ref, *, s, scale, heads):
    blk = SB_BLOCK
    row = lax.broadcasted_iota(jnp.int32, (blk, blk), 0)
    col = lax.broadcasted_iota(jnp.int32, (blk, blk), 1)
    before = col < row
    jrow = lax.broadcasted_iota(jnp.int32, (2 * blk, 2 * blk), 0) % blk
    scol = lax.broadcasted_iota(jnp.int32, (2 * blk, 2 * blk), 1)
    u2 = jnp.where((jrow > scol) | (scol >= blk), 1.0, 0.0).astype(BF16)

    def suffix(lk):
        hi = lk.astype(BF16)
        lo = (lk - hi.astype(F32)).astype(BF16)
        r = _dot(jnp.concatenate([hi, lo], axis=1), u2)
        return r[:, 0:blk], r[:, blk:2 * blk]

    head_lanes = [slice(hd * SB_HEAD, (hd + 1) * SB_HEAD) for hd in range(heads)]

    def qblock(qi, _):
        q0 = pl.multiple_of(qi * blk, blk)
        qbs = [q_ref[0, pl.ds(q0, blk), lanes] for lanes in head_lanes]

        def key_blocks(j0s, st, diag):
            z = [[lax.dot_general(qb, k_ref[0, pl.ds(j0, blk), lanes], _NT, preferred_element_type=F32) * scale
                  for qb, lanes in zip(qbs, head_lanes)] for j0 in j0s]
            sp = [[_softplus(x) for x in zs] for zs in z]
            lk = [[-x for x in sps] for sps in sp]
            if diag:
                lk = [[jnp.where(before, x, 0.0) for x in lks] for lks in lk]
            suf = [[suffix(x) for x in lks] for lks in lk]
            carry = [c for c, _ in st]
            acc = [a for _, a in st]
            for j0, zs, sps, sufs in zip(j0s, z, sp, suf):
                wts = [jnp.exp((zz - ss) + (later + c)) for zz, ss, (later, _), c in zip(zs, sps, sufs, carry)]
                if diag:
                    wts = [jnp.where(before, x, 0.0) for x in wts]
                acc = [a + _dot(x.astype(BF16), v_ref[0, pl.ds(j0, blk), lanes])
                       for a, x, lanes in zip(acc, wts, head_lanes)]
                carry = [c + total for c, (_, total) in zip(carry, sufs)]
            return tuple(zip(carry, acc))

        zeros = jnp.zeros((blk, blk), F32)
        st = key_blocks([q0], ((zeros, zeros),) * heads, True)

        def ksteps(it, st):
            ja = (qi - 1 - SB_KEYS_PER_STEP * it) * blk
            return key_blocks([pl.multiple_of(ja - n * blk, blk) for n in range(SB_KEYS_PER_STEP)], st, False)

        def kstep(it, st):
            return key_blocks([pl.multiple_of((qi % SB_KEYS_PER_STEP - 1 - it) * blk, blk)], st, False)

        st = lax.fori_loop(0, qi // SB_KEYS_PER_STEP, ksteps, st)
        st = lax.fori_loop(0, qi % SB_KEYS_PER_STEP, kstep, st)
        for hd, lanes in enumerate(head_lanes):
            o_ref[0, pl.ds(q0, blk), lanes] = st[hd][1].astype(o_ref.dtype)
        return 0

    lax.fori_loop(0, s // blk, qblock, 0)


def _sb_mixer(h, heads_per_step=8):
    b, s, c3 = h.shape
    c = c3 // 3
    w = heads_per_step * SB_HEAD
    groups = c // w
    return pl.pallas_call(
        functools.partial(_sb_kernel, s=s, scale=SB_HEAD ** -0.5, heads=heads_per_step),
        grid=(b, groups),
        in_specs=[pl.BlockSpec((1, s, w), lambda bi, gi: (bi, 0, gi)),
                  pl.BlockSpec((1, s, w), lambda bi, gi: (bi, 0, gi + groups)),
                  pl.BlockSpec((1, s, w), lambda bi, gi: (bi, 0, gi + 2 * groups))],
        out_specs=pl.BlockSpec((1, s, w), lambda bi, gi: (bi, 0, gi)),
        out_shape=jax.ShapeDtypeStruct((b, s, c), BF16),
        compiler_params=_params("parallel", "parallel"),
        name="sb_mixer",
    )(h, h, h)


def _rwkv_prep_kernel(h_ref, l_ref, mu_ref, mul_ref, w0_ref, wup_ref, a0_ref, aup_ref, gup_ref, kk_ref, ka_ref,
                      r_ref, lw_ref, kkr_ref, a_ref, kf_ref, v_ref, g_ref, hbuf, lbuf, *, ts, c):
    s = pl.program_id(1)
    c3 = 3 * c
    nl = l_ref.shape[2]

    @pl.when(s == 0)
    def _():
        hbuf[0:SUBLANES, :] = jnp.zeros((SUBLANES, c3), F32)
        lbuf[0:SUBLANES, :] = jnp.zeros((SUBLANES, nl), F32)

    @pl.when(s > 0)
    def _():
        hbuf[0:SUBLANES, :] = hbuf[ts:ts + SUBLANES, :]
        lbuf[0:SUBLANES, :] = lbuf[ts:ts + SUBLANES, :]

    h = h_ref[0]
    lo = l_ref[0]
    hbuf[SUBLANES:SUBLANES + ts, :] = h
    lbuf[SUBLANES:SUBLANES + ts, :] = lo
    h = h + (hbuf[SUBLANES - 1:SUBLANES - 1 + ts, :] - h) * mu_ref[...]
    lo = lo + (lbuf[SUBLANES - 1:SUBLANES - 1 + ts, :] - lo) * mul_ref[...]

    r = h[:, 0:c]
    k = h[:, c:2 * c]
    xw = lo[:, 0:LANES]
    xa = lo[:, LANES:2 * LANES]
    xg = lo[:, 2 * LANES:nl]

    w_log = -_softplus(-(w0_ref[...] + _dot(jnp.tanh(xw).astype(BF16), wup_ref[...]))) - 0.5
    a = jax.nn.sigmoid(a0_ref[...] + _dot(xa.astype(BF16), aup_ref[...]))
    r_ref[0] = r
    lw_ref[0] = -jnp.exp(w_log)
    kkr_ref[0] = k * kk_ref[...]
    a_ref[0] = a
    kf_ref[0] = k * (1.0 + (a - 1.0) * ka_ref[...])
    v_ref[0] = h[:, 2 * c:3 * c]
    g_ref[0] = _dot(jax.nn.sigmoid(xg).astype(BF16), gup_ref[...])


def _rwkv_prep(h, lora, mu, mul, w0, w_up, a0, a_up, g_up, k_k, k_a, ts=256):
    b, s, c3 = h.shape
    c = c3 // 3
    nl = lora.shape[2]
    nat = pl.BlockSpec((1, ts, c), lambda bi, si: (bi, si, 0))
    vec = lambda n: pl.BlockSpec((1, n), lambda bi, si: (0, 0))
    full = lambda a: pl.BlockSpec(a.shape, lambda bi, si: (0, 0))
    return pl.pallas_call(
        functools.partial(_rwkv_prep_kernel, ts=ts, c=c),
        grid=(b, s // ts),
        in_specs=[pl.BlockSpec((1, ts, c3), lambda bi, si: (bi, si, 0)),
                  pl.BlockSpec((1, ts, nl), lambda bi, si: (bi, si, 0)),
                  vec(c3), vec(nl), vec(c), full(w_up), vec(c), full(a_up), full(g_up), vec(c), vec(c)],
        out_specs=[nat] * 7,
        out_shape=[jax.ShapeDtypeStruct((b, s, c), F32)] * 7,
        scratch_shapes=[pltpu.VMEM((ts + SUBLANES, c3), F32), pltpu.VMEM((ts + SUBLANES, nl), F32)],
        compiler_params=_params("parallel", "arbitrary"),
        name="rwkv_prep",
    )(h, lora, mu.reshape(1, c3), mul.reshape(1, nl), w0.reshape(1, c), w_up, a0.reshape(1, c), a_up, g_up,
      k_k.reshape(1, c), k_a.reshape(1, c))


def _split(x):
    hi = x.astype(BF16)
    return hi, (x - hi.astype(F32)).astype(BF16)


def _dot3(a, b, dims=_NN):
    lhs = jnp.concatenate([a[0], a[1]], axis=1)
    zero = jnp.zeros_like(b[0])
    if dims is _NN:
        rhs = jnp.concatenate([jnp.concatenate([b[0], b[1]], axis=1),
                               jnp.concatenate([b[0], zero], axis=1)], axis=0)
    else:
        rhs = jnp.concatenate([jnp.concatenate([b[0], b[0]], axis=1),
                               jnp.concatenate([b[1], zero], axis=1)], axis=0)
    out = lax.dot_general(lhs, rhs, dims, preferred_element_type=F32)
    n = out.shape[1] // 2
    return out[:, :n] + out[:, n:]


def _rwkv_chunk_kernel(r_ref, lw_ref, kkr_ref, a_ref, kf_ref, v_ref, g_ref, rk_ref, lg_ref, lb_ref,
                       o_ref, state, *, tt, groups):
    n, t = RWKV_HEAD, RWKV_CHUNK
    t2 = 2 * t

    @pl.when(pl.program_id(2) == 0)
    def _():
        state[...] = jnp.zeros_like(state)

    first = lax.broadcasted_iota(jnp.int32, (t, LANES), 1) < n
    row = lax.broadcasted_iota(jnp.int32, (t2, t2), 0)
    col = lax.broadcasted_iota(jnp.int32, (t2, t2), 1)
    strict, incl = row > col, row >= col
    eye = jnp.where(row == col, 1.0, 0.0)
    tri3 = jnp.where(lax.broadcasted_iota(jnp.int32, (t, 3 * t), 0) >= lax.broadcasted_iota(jnp.int32, (t, 3 * t), 1) % t,
                     1.0, 0.0).astype(BF16)

    def head_sum(x):
        s0 = jnp.sum(jnp.where(first, x, 0.0), axis=1, keepdims=True)
        s1 = jnp.sum(jnp.where(first, 0.0, x), axis=1, keepdims=True)
        return jnp.where(first, s0, s1)

    def stack(x):
        return jnp.concatenate([jnp.where(first, x, 0.0), jnp.where(first, 0.0, x)], axis=0)

    cat = lambda *xs: jnp.concatenate(xs, axis=0)
    nch = tt // t
    units = [(slice(c * t, (c + 1) * t), slice(g * LANES, (g + 1) * LANES)) for g in range(groups) for c in range(nch)]
    each = lambda fn, *lists: [fn(*xs) for xs in zip(*lists)]

    def setup(rw, ln):
        r, lw, kkr, a, kf, v = (ref[0, rw, ln] for ref in (r_ref, lw_ref, kkr_ref, a_ref, kf_ref, v_ref))
        h1 = lw.astype(BF16)
        r1 = lw - h1.astype(F32)
        h2 = r1.astype(BF16)
        h3 = (r1 - h2.astype(F32)).astype(BF16)
        cum = _dot(tri3, cat(h1, h2, h3))
        cum_t = cum[t - 1:t, :]
        kk = kkr / jnp.maximum(jnp.sqrt(head_sum(kkr * kkr)), 1e-12)
        b = kk * a
        inv_p = jnp.exp(-cum)
        to_end = jnp.exp(cum_t - cum)
        ks = stack(kk * jnp.exp(cum - lw))
        kr = _split(cat(ks, stack(r * jnp.exp(cum))))
        bp = _split(cat(stack(b * inv_p), stack(kf * inv_p)))
        ends = _split(cat(stack(b * to_end), stack(kf * to_end)))
        bonus = head_sum(r * kf * rk_ref[:, ln]) * v
        return ks, kr, bp, ends, stack(v), jnp.exp(cum_t), bonus

    ks, kr, bp, ends, vs, decay_t, bonus = zip(*[setup(rw, ln) for rw, ln in units])
    gram = each(lambda x, y: _dot3(x, y, _NT), kr, bp)
    a_m = [jnp.where(strict, g[:t2, :t2], 0.0) for g in gram]
    catp = lambda x, y: (cat(x[0], y[0]), cat(x[1], y[1]))
    minv = [eye - x for x in a_m]
    pw = [_split(x) for x in a_m]
    pw = [_split(_dot3(x, x)) for x in pw]
    for _ in range(t.bit_length() - 3):
        both = each(lambda p, m: _dot3(catp(p, _split(m)), p), pw, minv)
        minv = each(lambda m, bo: m + bo[t2:], minv, both)
        pw = [_split(bo[:t2]) for bo in both]
    minv = each(lambda m, p: m + _dot3(_split(m), p), minv, pw)
    ms = [_split(m) for m in minv]
    vss = [_split(x) for x in vs]
    lv = each(lambda g, x: _dot3(_split(cat(jnp.where(strict, g[:t2, t2:], 0.0),
                                            jnp.where(incl, g[t2:, t2:], 0.0))), x), gram, vss)
    ds = [x[t2:] for x in lv]
    ktcs = each(lambda m, x, y: _dot3(m, _split(jnp.concatenate([x, y[:t2]], axis=1))), ms, ks, lv)
    kt = [_split(x[:, :LANES]) for x in ktcs]
    cs = [x[:, LANES:] for x in ktcs]
    lbs = [_split(jnp.where(incl, g[t2:, :t2], 0.0)) for g in gram]
    lhs = each(lambda k, x: (cat(k[0], x[0][t2:]), cat(k[1], x[1][t2:])), kt, kr)

    st = [state[g] for g in range(groups)]
    for c in range(nch):
        ids = [g * nch + c for g in range(groups)]
        x = [_dot3(lhs[i], _split(s_g), _NT) for i, s_g in zip(ids, st)]
        us = [-xx[:t2] - cs[i] for i, xx in zip(ids, x)]
        ys = [xx[t2:] + _dot3(lbs[i], _split(u)) + ds[i] for i, xx, u in zip(ids, x, us)]
        uv_t = [cat(u, vs[i]).T for i, u in zip(ids, us)]
        st = [s_g * decay_t[i] + _dot3(_split(m), ends[i]) for i, s_g, m in zip(ids, st, uv_t)]
        for i, yy in zip(ids, ys):
            rw, ln = units[i]
            y = yy[:t] + yy[t:]
            mu = head_sum(y) * (1.0 / n)
            d = y - mu
            var = head_sum(d * d) * (1.0 / n)
            y = d * lax.rsqrt(var + LNX_EPS) * lg_ref[:, ln] + lb_ref[:, ln]
            o_ref[0, rw, ln] = ((y + bonus[i]) * g_ref[0, rw, ln]).astype(o_ref.dtype)
    for g in range(groups):
        state[g] = st[g]


def _rwkv_chunks(r, lw, kkr, a, kf, v, g, r_k, lnx_g, lnx_b, tt=128, groups=4):
    b, s, c = v.shape
    tt = _blk(s, tt)
    cw = groups * LANES
    nat = pl.BlockSpec((1, tt, cw), lambda bi, pi, si: (bi, si, pi))
    vec = pl.BlockSpec((1, cw), lambda bi, pi, si: (0, pi))
    return pl.pallas_call(
        functools.partial(_rwkv_chunk_kernel, tt=tt, groups=groups),
        grid=(b, c // cw, s // tt),
        in_specs=[nat] * 7 + [vec] * 3,
        out_specs=nat,
        out_shape=jax.ShapeDtypeStruct((b, s, c), BF16),
        scratch_shapes=[pltpu.VMEM((groups, LANES, LANES), F32)],
        compiler_params=_params("parallel", "parallel", "arbitrary"),
        name="rwkv_chunks",
    )(r, lw, kkr, a, kf, v, g, r_k.reshape(1, c), lnx_g.reshape(1, c), lnx_b.reshape(1, c))


def _pad_rows(w, rows):
    return jnp.pad(w, ((0, rows - w.shape[0]), (0, 0)))


def _rwkv_mixer(h, lora, mu, w0, w_up, a0, a_up, g_up, k_k, k_a, r_k, lnx_g, lnx_b):
    c = h.shape[2] // 3
    mu_main = mu[:3 * c]
    mu_l = _lora_slots(mu[None, 3 * c:])[0]
    r, lw, kkr, a, kf, v, g = _rwkv_prep(
        h, lora, mu_main, mu_l, w0, _pad_rows(w_up, LANES).astype(BF16), a0,
        _pad_rows(a_up, LANES).astype(BF16), _pad_rows(g_up, 2 * LANES).astype(BF16), k_k, k_a)
    return _rwkv_chunks(r, lw, kkr, a, kf, v, g, r_k.reshape(-1), lnx_g, lnx_b)


def _lora_slots(w):
    xw = w[:, :DECAY_LORA]
    xa = w[:, DECAY_LORA:DECAY_LORA + ICLR_LORA]
    xg = w[:, DECAY_LORA + ICLR_LORA:]
    padto = lambda t, n: jnp.pad(t, ((0, 0), (0, n - t.shape[1])))
    return jnp.concatenate([padto(xw, LANES), padto(xa, LANES), padto(xg, 2 * LANES)], axis=1)


def _ffn(x2d, norm_g, w_in, w_out, layer):
    xn = _rmsnorm(x2d, norm_g, BF16)
    hid, w_out_bf16 = _matmul_swiglu(xn, w_in, w_out, layer, BF16)
    return _matmul_residual(hid, w_out_bf16, x2d, 0.5)


def kernel(x, ffn1_norm, ffn1_w_in, ffn1_w_out, mix_norm, mix_w_in, conv_w, conv_b, conv_ln_g, conv_ln_b, rwkv_mu, rwkv_w0, rwkv_w_up, rwkv_a0, rwkv_a_up, rwkv_g_up, rwkv_k_k, rwkv_k_a, rwkv_r_k, rwkv_lnx_g, rwkv_lnx_b, sgu_ln_g, sgu_ln_b, sgu_w_s, sgu_b_s, mix_w_out, ffn2_norm, ffn2_w_in, ffn2_w_out, final_norm):
    b, s, d = x.shape
    depth = ffn1_norm.shape[0]
    c = conv_w.shape[2]
    o_lora = 5 * c
    o_sgu = o_lora + DECAY_LORA + ICLR_LORA + GATE_LORA
    w_mix_t = jnp.swapaxes(mix_w_in, 1, 2)

    x2d = x.reshape(b * s, d)
    for l in range(depth):
        x2d = _ffn(x2d, ffn1_norm[l], ffn1_w_in, ffn1_w_out, l)

        xn = _rmsnorm(x2d, mix_norm[l], BF16)
        mm = lambda wt, lyr, row0, n, dt: _matmul_nt(xn, wt, lyr, row0, n, dt).reshape(b, s, n)
        w_lora_t = _lora_slots(mix_w_in[l, :, o_lora:o_sgu]).T[None]
        h_conv = mm(w_mix_t, l, 0, 2 * c, F32)
        h_rkv = mm(w_mix_t, l, 2 * c, 3 * c, F32)
        h_lora = mm(w_lora_t, 0, 0, w_lora_t.shape[1], F32)
        h_sgu = mm(w_mix_t, l, o_sgu, 2 * c, F32)
        h_sb = mm(w_mix_t, l, o_sgu + 2 * c, 3 * c, BF16)

        y_conv = _conv_mixer(h_conv, conv_w[l], conv_b[l], conv_ln_g[l], conv_ln_b[l])
        y_rwkv = _rwkv_mixer(h_rkv, h_lora, rwkv_mu[l], rwkv_w0[l], rwkv_w_up[l], rwkv_a0[l], rwkv_a_up[l],
                             rwkv_g_up[l], rwkv_k_k[l], rwkv_k_a[l], rwkv_r_k[l], rwkv_lnx_g[l], rwkv_lnx_b[l])
        y_sgu = _sgu_mixer(h_sgu, sgu_ln_g[l], sgu_ln_b[l], sgu_w_s[l], sgu_b_s[l])
        y_sb = _sb_mixer(h_sb)
        groups = [t.reshape(b * s, c) for t in (y_conv, y_rwkv, y_sgu, y_sb)]
        x2d = _matmul_groups_residual(groups, mix_w_out, l, x2d)

        x2d = _ffn(x2d, ffn2_norm[l], ffn2_w_in, ffn2_w_out, l)
    return _rmsnorm(x2d, final_norm, F32).reshape(b, s, d)
```

```python
import functools

import jax
import jax.numpy as jnp
from jax import lax
from jax.experimental import pallas as pl
from jax.experimental.pallas import tpu as pltpu

LANES = 128
SUBLANES = 8
VMEM_LIMIT_BYTES = 52 * 1024 * 1024
VMEM_LIMIT_WIDE_BYTES = 57 * 1024 * 1024

CHUNK = 64
CONV_K = 31
CONV_HALO = 32
RWKV_HEAD = 64
RWKV_CHUNK = 64
DECAY_LORA = 64
ICLR_LORA = 64
GATE_LORA = 160
LNX_EPS = 64e-5
SGU_BLOCK = 128
SGU_HEAD = 128
SB_HEAD = 128
SB_BLOCK = 128
SB_KEYS_PER_STEP = 4
EPS = 1e-6

BF16 = jnp.bfloat16
F32 = jnp.float32


def _params(*sem, vmem=VMEM_LIMIT_BYTES):
    return pltpu.CompilerParams(dimension_semantics=sem, vmem_limit_bytes=vmem)


_NN = (((1,), (0,)), ((), ()))
_NT = (((1,), (1,)), ((), ()))


def _dot(a, b):
    return jnp.dot(a, b, preferred_element_type=F32)


def _blk(n, pref):
    if n <= pref:
        return n
    for cand in range(pref - pref % LANES, 0, -LANES):
        if n % cand == 0:
            return cand
    return n


def _rmsnorm_kernel(x_ref, g_ref, o_ref):
    x = x_ref[...]
    ms = jnp.mean(x * x, axis=-1, keepdims=True)
    o_ref[...] = (x * lax.rsqrt(ms + EPS) * g_ref[...]).astype(o_ref.dtype)


def _rmsnorm(x, g, out_dtype, bm=256):
    m, d = x.shape
    bm = _blk(m, bm)
    return pl.pallas_call(
        _rmsnorm_kernel,
        grid=(m // bm,),
        in_specs=[pl.BlockSpec((bm, d), lambda i: (i, 0)),
                  pl.BlockSpec((1, d), lambda i: (0, 0))],
        out_specs=pl.BlockSpec((bm, d), lambda i: (i, 0)),
        out_shape=jax.ShapeDtypeStruct((m, d), out_dtype),
        compiler_params=_params("parallel"),
        name="rmsnorm",
    )(x, g.reshape(1, d))


def _mm_nt_kernel(a_ref, w_ref, o_ref):
    o_ref[...] = lax.dot_general(a_ref[...], w_ref[0].astype(BF16), _NT,
                                 preferred_element_type=F32).astype(o_ref.dtype)


def _matmul_nt(a, wt, layer, row0, n, out_dtype, bm=2048, bn=256):
    m, k = a.shape
    bm, bn = _blk(m, bm), _blk(n, bn)
    assert row0 % SUBLANES == 0
    w_spec = pl.BlockSpec((pl.Element(1), pl.Element(bn), pl.Element(k)),
                          lambda i, j: (layer, pl.multiple_of(row0 + j * bn, SUBLANES), 0))
    return pl.pallas_call(
        _mm_nt_kernel,
        grid=(m // bm, n // bn),
        in_specs=[pl.BlockSpec((bm, k), lambda i, j: (i, 0)), w_spec],
        out_specs=pl.BlockSpec((bm, bn), lambda i, j: (i, j)),
        out_shape=jax.ShapeDtypeStruct((m, n), out_dtype),
        compiler_params=_params("parallel", "parallel", vmem=VMEM_LIMIT_WIDE_BYTES),
        name="matmul_nt",
    )(a, wt)


def _mm_swiglu_kernel(a_ref, wg_ref, wu_ref, wo_ref, o_ref, wo_bf16_ref):
    a = a_ref[...]
    gate = _dot(a, wg_ref[...].astype(BF16))
    up = _dot(a, wu_ref[...].astype(BF16))
    o_ref[...] = (jax.nn.silu(gate) * up).astype(o_ref.dtype)
    wo_bf16_ref[...] = wo_ref[...].astype(BF16)


def _matmul_swiglu(a, w, w_out, layer, out_dtype, bm=2048, bn=256):
    m, k = a.shape
    f = w.shape[2] // 2
    bm, bn = _blk(m, bm), _blk(f, bn)
    nj = f // bn
    steps = (m // bm) * nj
    slab = w_out.shape[1] // steps
    assert slab * steps == w_out.shape[1] and slab % (2 * SUBLANES) == 0
    n = w_out.shape[2]
    return pl.pallas_call(
        _mm_swiglu_kernel,
        grid=(m // bm, nj),
        in_specs=[pl.BlockSpec((bm, k), lambda i, j: (i, 0)),
                  pl.BlockSpec((None, k, bn), lambda i, j: (layer, 0, j)),
                  pl.BlockSpec((None, k, bn), lambda i, j: (layer, 0, j + nj)),
                  pl.BlockSpec((None, slab, n), lambda i, j: (layer, i * nj + j, 0))],
        out_specs=[pl.BlockSpec((bm, bn), lambda i, j: (i, j)),
                   pl.BlockSpec((slab, n), lambda i, j: (i * nj + j, 0))],
        out_shape=[jax.ShapeDtypeStruct((m, f), out_dtype),
                   jax.ShapeDtypeStruct((w_out.shape[1], n), BF16)],
        compiler_params=_params("parallel", "arbitrary", vmem=VMEM_LIMIT_WIDE_BYTES),
        name="matmul_swiglu",
    )(a, w, w, w_out)


def _mm_res_kernel(a_ref, w_ref, r_ref, o_ref, *, scale):
    o_ref[...] = r_ref[...] + scale * _dot(a_ref[...], w_ref[...])


def _matmul_residual(a, w, res, scale, bm=512, bn=512):
    m, k = a.shape
    n = w.shape[1]
    bm, bn = _blk(m, bm), _blk(n, bn)
    return pl.pallas_call(
        functools.partial(_mm_res_kernel, scale=scale),
        grid=(m // bm, n // bn),
        in_specs=[pl.BlockSpec((bm, k), lambda i, j: (i, 0)),
                  pl.BlockSpec((k, bn), lambda i, j: (0, j)),
                  pl.BlockSpec((bm, bn), lambda i, j: (i, j))],
        out_specs=pl.BlockSpec((bm, bn), lambda i, j: (i, j)),
        out_shape=jax.ShapeDtypeStruct((m, n), F32),
        compiler_params=_params("parallel", "parallel"),
        name="matmul_residual",
    )(a, w, res)


def _mm_groups_res_kernel(a0_ref, a1_ref, a2_ref, a3_ref, w_ref, r_ref, o_ref):
    kg = a0_ref.shape[1]
    acc = r_ref[...]
    for g, a_ref in enumerate((a0_ref, a1_ref, a2_ref, a3_ref)):
        acc = acc + _dot(a_ref[...], w_ref[g * kg:(g + 1) * kg, :].astype(BF16))
    o_ref[...] = acc


def _matmul_groups_residual(groups, w, layer, res, bm=1024, bn=512):
    m, kg = groups[0].shape
    _, k, n = w.shape
    bm, bn = _blk(m, bm), _blk(n, bn)
    a_spec = pl.BlockSpec((bm, kg), lambda i, j: (i, 0))
    return pl.pallas_call(
        _mm_groups_res_kernel,
        grid=(m // bm, n // bn),
        in_specs=[a_spec, a_spec, a_spec, a_spec,
                  pl.BlockSpec((None, k, bn), lambda i, j: (layer, 0, j)),
                  pl.BlockSpec((bm, bn), lambda i, j: (i, j))],
        out_specs=pl.BlockSpec((bm, bn), lambda i, j: (i, j)),
        out_shape=jax.ShapeDtypeStruct((m, n), F32),
        compiler_params=_params("parallel", "parallel"),
        name="matmul_groups_residual",
    )(*groups, w, res)


def _conv_kernel(h_ref, w_ref, b_ref, g_ref, beta_ref, o_ref, ybuf, cbuf, *, ts, c):
    s = pl.program_id(1)

    @pl.when(s == 0)
    def _():
        ybuf[0:CONV_HALO, :] = jnp.zeros((CONV_HALO, c), F32)
        ybuf[CONV_HALO + ts:CONV_HALO + ts + SUBLANES, :] = jnp.zeros((SUBLANES, c), F32)

    @pl.when(s > 0)
    def _():
        ybuf[0:CONV_HALO, :] = ybuf[ts:ts + CONV_HALO, :]

    a = h_ref[0, :, 0:c]
    gate = h_ref[0, :, c:2 * c]
    ybuf[CONV_HALO:CONV_HALO + ts, :] = a * jax.nn.sigmoid(gate)

    row0 = CONV_HALO - (CONV_K - 1)
    span = ts + SUBLANES
    for cb in range(c // LANES):
        lanes = slice(cb * LANES, (cb + 1) * LANES)
        out = jnp.broadcast_to(b_ref[:, lanes], (ts, LANES))
        for p in range(SUBLANES):
            part = None
            for j in range(CONV_K):
                if (row0 + j) % SUBLANES == p:
                    r0 = row0 + j - p
                    term = w_ref[j:j + 1, lanes] * ybuf[r0:r0 + span, lanes]
                    part = term if part is None else part + term
            out = out + part[p:p + ts, :]
        cbuf[:, lanes] = out

    y = cbuf[...]
    mu = jnp.mean(y, axis=-1, keepdims=True)
    d = y - mu
    var = jnp.mean(d * d, axis=-1, keepdims=True)
    y = d * lax.rsqrt(var + EPS) * g_ref[...] + beta_ref[...]
    o_ref[0] = jax.nn.silu(y).astype(o_ref.dtype)


def _conv_mixer(h, w_dw, b_dw, ln_g, ln_b, ts=128):
    b, s, _ = h.shape
    c = w_dw.shape[1]
    c2 = 2 * c
    vec = pl.BlockSpec((1, c), lambda bi, si: (0, 0))
    return pl.pallas_call(
        functools.partial(_conv_kernel, ts=ts, c=c),
        grid=(b, s // ts),
        in_specs=[pl.BlockSpec((1, ts, c2), lambda bi, si: (bi, si, 0)),
                  pl.BlockSpec((CONV_K, c), lambda bi, si: (0, 0)),
                  vec, vec, vec],
        out_specs=pl.BlockSpec((1, ts, c), lambda bi, si: (bi, si, 0)),
        out_shape=jax.ShapeDtypeStruct((b, s, c), BF16),
        scratch_shapes=[pltpu.VMEM((ts + CONV_HALO + SUBLANES, c), F32), pltpu.VMEM((ts, c), F32)],
        compiler_params=_params("parallel", "arbitrary"),
        name="conv_mixer",
    )(h, w_dw, b_dw.reshape(1, c), ln_g.reshape(1, c), ln_b.reshape(1, c))


def _sgu_kernel(h_ref, g_ref, beta_ref, ws_ref, bs_ref, o_ref, *, ts, c):
    hh = jax.nn.gelu(h_ref[0])
    u = hh[:, 0:c]
    v = hh[:, c:2 * c]
    mu = jnp.mean(v, axis=-1, keepdims=True)
    d = v - mu
    var = jnp.mean(d * d, axis=-1, keepdims=True)
    v = (d * lax.rsqrt(var + EPS) * g_ref[...] + beta_ref[...]).astype(BF16)
    row = lax.broadcasted_iota(jnp.int32, (SGU_BLOCK, SGU_BLOCK), 0)
    col = lax.broadcasted_iota(jnp.int32, (SGU_BLOCK, SGU_BLOCK), 1)
    chunk_causal = (col // CHUNK) <= (row // CHUNK)
    for hd in range(c // SGU_HEAD):
        lanes = slice(hd * SGU_HEAD, (hd + 1) * SGU_HEAD)
        w = jnp.where(chunk_causal, ws_ref[hd], 0.0).astype(BF16)
        for blk in range(ts // SGU_BLOCK):
            rows = slice(blk * SGU_BLOCK, (blk + 1) * SGU_BLOCK)
            sv = _dot(w, v[rows, lanes]) + bs_ref[:, lanes]
            o_ref[0, rows, lanes] = (u[rows, lanes] * sv).astype(o_ref.dtype)


def _sgu_mixer(h, ln_g, ln_b, w_s, b_s, ts=256):
    b, s, c2 = h.shape
    c = c2 // 2
    heads = w_s.shape[0]
    bias = jnp.repeat(b_s.T, c // heads, axis=1)
    vec = pl.BlockSpec((1, c), lambda bi, si: (0, 0))
    return pl.pallas_call(
        functools.partial(_sgu_kernel, ts=ts, c=c),
        grid=(b, s // ts),
        in_specs=[pl.BlockSpec((1, ts, c2), lambda bi, si: (bi, si, 0)),
                  vec, vec,
                  pl.BlockSpec((heads, SGU_BLOCK, SGU_BLOCK), lambda bi, si: (0, 0, 0)),
                  pl.BlockSpec((SGU_BLOCK, c), lambda bi, si: (0, 0))],
        out_specs=pl.BlockSpec((1, ts, c), lambda bi, si: (bi, si, 0)),
        out_shape=jax.ShapeDtypeStruct((b, s, c), BF16),
        compiler_params=_params("parallel", "parallel"),
        name="sgu_mixer",
    )(h, ln_g.reshape(1, c), ln_b.reshape(1, c), w_s, bias)


def _softplus(z):
    return jnp.maximum(z, 0.0) + jnp.log1p(jnp.exp(-jnp.abs(z)))


def _sb_kernel(q_ref, k_ref, v_ref, o_ref, *, s, scale, heads):
    blk = SB_BLOCK
    row = lax.broadcasted_iota(jnp.int32, (blk, blk), 0)
    col = lax.broadcasted_iota(jnp.int32, (blk, blk), 1)
    before = col < row
    jrow = lax.broadcasted_iota(jnp.int32, (2 * blk, 2 * blk), 0) % blk
    scol = lax.broadcasted_iota(jnp.int32, (2 * blk, 2 * blk), 1)
    u2 = jnp.where((jrow > scol) | (scol >= blk), 1.0, 0.0).astype(BF16)

    def suffix(lk):
        hi = lk.astype(BF16)
        lo = (lk - hi.astype(F32)).astype(BF16)
        r = _dot(jnp.concatenate([hi, lo], axis=1), u2)
        return r[:, 0:blk], r[:, blk:2 * blk]

    head_lanes = [slice(hd * SB_HEAD, (hd + 1) * SB_HEAD) for hd in range(heads)]

    def qblock(qi, _):
        q0 = pl.multiple_of(qi * blk, blk)
        qbs = [q_ref[0, pl.ds(q0, blk), lanes] for lanes in head_lanes]

        def key_blocks(j0s, st, diag):
            z = [[lax.dot_general(qb, k_ref[0, pl.ds(j0, blk), lanes], _NT, preferred_element_type=F32) * scale
                  for qb, lanes in zip(qbs, head_lanes)] for j0 in j0s]
            sp = [[_softplus(x) for x in zs] for zs in z]
            lk = [[-x for x in sps] for sps in sp]
            if diag:
                lk = [[jnp.where(before, x, 0.0) for x in lks] for lks in lk]
            suf = [[suffix(x) for x in lks] for lks in lk]
            carry = [c for c, _ in st]
            acc = [a for _, a in st]
            for j0, zs, sps, sufs in zip(j0s, z, sp, suf):
                wts = [jnp.exp((zz - ss) + (later + c)) for zz, ss, (later, _), c in zip(zs, sps, sufs, carry)]
                if diag:
                    wts = [jnp.where(before, x, 0.0) for x in wts]
                acc = [a + _dot(x.astype(BF16), v_ref[0, pl.ds(j0, blk), lanes])
                       for a, x, lanes in zip(acc, wts, head_lanes)]
                carry = [c + total for c, (_, total) in zip(carry, sufs)]
            return tuple(zip(carry, acc))

        zeros = jnp.zeros((blk, blk), F32)
        st = key_blocks([q0], ((zeros, zeros),) * heads, True)

        def ksteps(it, st):
            ja = (qi - 1 - SB_KEYS_PER_STEP * it) * blk
            return key_blocks([pl.multiple_of(ja - n * blk, blk) for n in range(SB_KEYS_PER_STEP)], st, False)

        def kstep(it, st):
            return key_blocks([pl.multiple_of((qi % SB_KEYS_PER_STEP - 1 - it) * blk, blk)], st, False)

        st = lax.fori_loop(0, qi // SB_KEYS_PER_STEP, ksteps, st)
        st = lax.fori_loop(0, qi % SB_KEYS_PER_STEP, kstep, st)
        for hd, lanes in enumerate(head_lanes):
            o_ref[0, pl.ds(q0, blk), lanes] = st[hd][1].astype(o_ref.dtype)
        return 0

    lax.fori_loop(0, s // blk, qblock, 0)


def _sb_mixer(h, heads_per_step=8):
    b, s, c3 = h.shape
    c = c3 // 3
    w = heads_per_step * SB_HEAD
    groups = c // w
    return pl.pallas_call(
        functools.partial(_sb_kernel, s=s, scale=SB_HEAD ** -0.5, heads=heads_per_step),
        grid=(b, groups),
        in_specs=[pl.BlockSpec((1, s, w), lambda bi, gi: (bi, 0, gi)),
                  pl.BlockSpec((1, s, w), lambda bi, gi: (bi, 0, gi + groups)),
                  pl.BlockSpec((1, s, w), lambda bi, gi: (bi, 0, gi + 2 * groups))],
        out_specs=pl.BlockSpec((1, s, w), lambda bi, gi: (bi, 0, gi)),
        out_shape=jax.ShapeDtypeStruct((b, s, c), BF16),
        compiler_params=_params("parallel", "parallel"),
        name="sb_mixer",
    )(h, h, h)


def _rwkv_prep_kernel(h_ref, l_ref, mu_ref, mul_ref, w0_ref, wup_ref, a0_ref, aup_ref, gup_ref, kk_ref, ka_ref,
                      r_ref, lw_ref, kkr_ref, a_ref, kf_ref, v_ref, g_ref, hbuf, lbuf, *, ts, c):
    s = pl.program_id(1)
    c3 = 3 * c
    nl = l_ref.shape[2]

    @pl.when(s == 0)
    def _():
        hbuf[0:SUBLANES, :] = jnp.zeros((SUBLANES, c3), F32)
        lbuf[0:SUBLANES, :] = jnp.zeros((SUBLANES, nl), F32)

    @pl.when(s > 0)
    def _():
        hbuf[0:SUBLANES, :] = hbuf[ts:ts + SUBLANES, :]
        lbuf[0:SUBLANES, :] = lbuf[ts:ts + SUBLANES, :]

    h = h_ref[0]
    lo = l_ref[0]
    hbuf[SUBLANES:SUBLANES + ts, :] = h
    lbuf[SUBLANES:SUBLANES + ts, :] = lo
    h = h + (hbuf[SUBLANES - 1:SUBLANES - 1 + ts, :] - h) * mu_ref[...]
    lo = lo + (lbuf[SUBLANES - 1:SUBLANES - 1 + ts, :] - lo) * mul_ref[...]

    r = h[:, 0:c]
    k = h[:, c:2 * c]
    xw = lo[:, 0:LANES]
    xa = lo[:, LANES:2 * LANES]
    xg = lo[:, 2 * LANES:nl]

    w_log = -_softplus(-(w0_ref[...] + _dot(jnp.tanh(xw).astype(BF16), wup_ref[...]))) - 0.5
    a = jax.nn.sigmoid(a0_ref[...] + _dot(xa.astype(BF16), aup_ref[...]))
    r_ref[0] = r
    lw_ref[0] = -jnp.exp(w_log)
    kkr_ref[0] = k * kk_ref[...]
    a_ref[0] = a
    kf_ref[0] = k * (1.0 + (a - 1.0) * ka_ref[...])
    v_ref[0] = h[:, 2 * c:3 * c]
    g_ref[0] = _dot(jax.nn.sigmoid(xg).astype(BF16), gup_ref[...])


def _rwkv_prep(h, col0, lora, mu, mul, w0, w_up, a0, a_up, g_up, k_k, k_a, ts=256):
    b, s, _ = h.shape
    c = w0.shape[0]
    c3 = 3 * c
    nl = lora.shape[2]
    nat = pl.BlockSpec((1, ts, c), lambda bi, si: (bi, si, 0))
    vec = lambda n: pl.BlockSpec((1, n), lambda bi, si: (0, 0))
    full = lambda a: pl.BlockSpec(a.shape, lambda bi, si: (0, 0))
    return pl.pallas_call(
        functools.partial(_rwkv_prep_kernel, ts=ts, c=c),
        grid=(b, s // ts),
        in_specs=[pl.BlockSpec((pl.Element(1), pl.Element(ts), pl.Element(c3)),
                               lambda bi, si: (bi, pl.multiple_of(si * ts, ts), col0)),
                  pl.BlockSpec((1, ts, nl), lambda bi, si: (bi, si, 0)),
                  vec(c3), vec(nl), vec(c), full(w_up), vec(c), full(a_up), full(g_up), vec(c), vec(c)],
        out_specs=[nat] * 7,
        out_shape=[jax.ShapeDtypeStruct((b, s, c), F32)] * 7,
        scratch_shapes=[pltpu.VMEM((ts + SUBLANES, c3), F32), pltpu.VMEM((ts + SUBLANES, nl), F32)],
        compiler_params=_params("parallel", "arbitrary"),
        name="rwkv_prep",
    )(h, lora, mu.reshape(1, c3), mul.reshape(1, nl), w0.reshape(1, c), w_up, a0.reshape(1, c), a_up, g_up,
      k_k.reshape(1, c), k_a.reshape(1, c))


def _split(x):
    hi = x.astype(BF16)
    return hi, (x - hi.astype(F32)).astype(BF16)


def _dot3(a, b, dims=_NN):
    lhs = jnp.concatenate([a[0], a[1]], axis=1)
    zero = jnp.zeros_like(b[0])
    if dims is _NN:
        rhs = jnp.concatenate([jnp.concatenate([b[0], b[1]], axis=1),
                               jnp.concatenate([b[0], zero], axis=1)], axis=0)
    else:
        rhs = jnp.concatenate([jnp.concatenate([b[0], b[0]], axis=1),
                               jnp.concatenate([b[1], zero], axis=1)], axis=0)
    out = lax.dot_general(lhs, rhs, dims, preferred_element_type=F32)
    n = out.shape[1] // 2
    return out[:, :n] + out[:, n:]


def _rwkv_chunk_kernel(r_ref, lw_ref, kkr_ref, a_ref, kf_ref, v_ref, g_ref, rk_ref, lg_ref, lb_ref,
                       o_ref, state, *, tt, groups):
    n, t = RWKV_HEAD, RWKV_CHUNK
    t2 = 2 * t

    @pl.when(pl.program_id(2) == 0)
    def _():
        state[...] = jnp.zeros_like(state)

    first = lax.broadcasted_iota(jnp.int32, (t, LANES), 1) < n
    row = lax.broadcasted_iota(jnp.int32, (t2, t2), 0)
    col = lax.broadcasted_iota(jnp.int32, (t2, t2), 1)
    strict, incl = row > col, row >= col
    eye = jnp.where(row == col, 1.0, 0.0)
    tri3 = jnp.where(lax.broadcasted_iota(jnp.int32, (t, 3 * t), 0) >= lax.broadcasted_iota(jnp.int32, (t, 3 * t), 1) % t,
                     1.0, 0.0).astype(BF16)

    def head_sum(x):
        s0 = jnp.sum(jnp.where(first, x, 0.0), axis=1, keepdims=True)
        s1 = jnp.sum(jnp.where(first, 0.0, x), axis=1, keepdims=True)
        return jnp.where(first, s0, s1)

    def stack(x):
        return jnp.concatenate([jnp.where(first, x, 0.0), jnp.where(first, 0.0, x)], axis=0)

    cat = lambda *xs: jnp.concatenate(xs, axis=0)
    nch = tt // t
    units = [(slice(c * t, (c + 1) * t), slice(g * LANES, (g + 1) * LANES)) for g in range(groups) for c in range(nch)]
    each = lambda fn, *lists: [fn(*xs) for xs in zip(*lists)]

    def setup(rw, ln):
        r, lw, kkr, a, kf, v = (ref[0, rw, ln] for ref in (r_ref, lw_ref, kkr_ref, a_ref, kf_ref, v_ref))
        h1 = lw.astype(BF16)
        r1 = lw - h1.astype(F32)
        h2 = r1.astype(BF16)
        h3 = (r1 - h2.astype(F32)).astype(BF16)
        cum = _dot(tri3, cat(h1, h2, h3))
        cum_t = cum[t - 1:t, :]
        kk = kkr / jnp.maximum(jnp.sqrt(head_sum(kkr * kkr)), 1e-12)
        b = kk * a
        inv_p = jnp.exp(-cum)
        to_end = jnp.exp(cum_t - cum)
        ks = stack(kk * jnp.exp(cum - lw))
        kr = _split(cat(ks, stack(r * jnp.exp(cum))))
        bp = _split(cat(stack(b * inv_p), stack(kf * inv_p)))
        ends = _split(cat(stack(b * to_end), stack(kf * to_end)))
        bonus = head_sum(r * kf * rk_ref[:, ln]) * v
        return ks, kr, bp, ends, stack(v), jnp.exp(cum_t), bonus

    ks, kr, bp, ends, vs, decay_t, bonus = zip(*[setup(rw, ln) for rw, ln in units])
    gram = each(lambda x, y: _dot3(x, y, _NT), kr, bp)
    a_m = [jnp.where(strict, g[:t2, :t2], 0.0) for g in gram]
    catp = lambda x, y: (cat(x[0], y[0]), cat(x[1], y[1]))
    minv = [eye - x for x in a_m]
    pw = [_split(x) for x in a_m]
    pw = [_split(_dot3(x, x)) for x in pw]
    for _ in range(t.bit_length() - 3):
        both = each(lambda p, m: _dot3(catp(p, _split(m)), p), pw, minv)
        minv = each(lambda m, bo: m + bo[t2:], minv, both)
        pw = [_split(bo[:t2]) for bo in both]
    minv = each(lambda m, p: m + _dot3(_split(m), p), minv, pw)
    ms = [_split(m) for m in minv]
    vss = [_split(x) for x in vs]
    lv = each(lambda g, x: _dot3(_split(cat(jnp.where(strict, g[:t2, t2:], 0.0),
                                            jnp.where(incl, g[t2:, t2:], 0.0))), x), gram, vss)
    ds = [x[t2:] for x in lv]
    ktcs = each(lambda m, x, y: _dot3(m, _split(jnp.concatenate([x, y[:t2]], axis=1))), ms, ks, lv)
    kt = [_split(x[:, :LANES]) for x in ktcs]
    cs = [x[:, LANES:] for x in ktcs]
    lbs = [_split(jnp.where(incl, g[t2:, :t2], 0.0)) for g in gram]
    lhs = each(lambda k, x: (cat(k[0], x[0][t2:]), cat(k[1], x[1][t2:])), kt, kr)

    st = [state[g] for g in range(groups)]
    for c in range(nch):
        ids = [g * nch + c for g in range(groups)]
        x = [_dot3(lhs[i], _split(s_g), _NT) for i, s_g in zip(ids, st)]
        us = [-xx[:t2] - cs[i] for i, xx in zip(ids, x)]
        ys = [xx[t2:] + _dot3(lbs[i], _split(u)) + ds[i] for i, xx, u in zip(ids, x, us)]
        uv_t = [cat(u, vs[i]).T for i, u in zip(ids, us)]
        st = [s_g * decay_t[i] + _dot3(_split(m), ends[i]) for i, s_g, m in zip(ids, st, uv_t)]
        for i, yy in zip(ids, ys):
            rw, ln = units[i]
            y = yy[:t] + yy[t:]
            mu = head_sum(y) * (1.0 / n)
            d = y - mu
            var = head_sum(d * d) * (1.0 / n)
            y = d * lax.rsqrt(var + LNX_EPS) * lg_ref[:, ln] + lb_ref[:, ln]
            o_ref[0, rw, ln] = ((y + bonus[i]) * g_ref[0, rw, ln]).astype(o_ref.dtype)
    for g in range(groups):
        state[g] = st[g]


def _rwkv_chunks(r, lw, kkr, a, kf, v, g, r_k, lnx_g, lnx_b, tt=128, groups=4):
    b, s, c = v.shape
    tt = _blk(s, tt)
    cw = groups * LANES
    nat = pl.BlockSpec((1, tt, cw), lambda bi, pi, si: (bi, si, pi))
    vec = pl.BlockSpec((1, cw), lambda bi, pi, si: (0, pi))
    return pl.pallas_call(
        functools.partial(_rwkv_chunk_kernel, tt=tt, groups=groups),
        grid=(b, c // cw, s // tt),
        in_specs=[nat] * 7 + [vec] * 3,
        out_specs=nat,
        out_shape=jax.ShapeDtypeStruct((b, s, c), BF16),
        scratch_shapes=[pltpu.VMEM((groups, LANES, LANES), F32)],
        compiler_params=_params("parallel", "parallel", "arbitrary"),
        name="rwkv_chunks",
    )(r, lw, kkr, a, kf, v, g, r_k.reshape(1, c), lnx_g.reshape(1, c), lnx_b.reshape(1, c))


def _pad_rows(w, rows):
    return jnp.pad(w, ((0, rows - w.shape[0]), (0, 0)))


def _rwkv_mixer(h, col0, lora, mu, w0, w_up, a0, a_up, g_up, k_k, k_a, r_k, lnx_g, lnx_b):
    c = w0.shape[0]
    mu_main = mu[:3 * c]
    mu_l = _lora_slots(mu[None, 3 * c:])[0]
    r, lw, kkr, a, kf, v, g = _rwkv_prep(
        h, col0, lora, mu_main, mu_l, w0, _pad_rows(w_up, LANES).astype(BF16), a0,
        _pad_rows(a_up, LANES).astype(BF16), _pad_rows(g_up, 2 * LANES).astype(BF16), k_k, k_a)
    return _rwkv_chunks(r, lw, kkr, a, kf, v, g, r_k.reshape(-1), lnx_g, lnx_b)


def _lora_slots(w):
    xw = w[:, :DECAY_LORA]
    xa = w[:, DECAY_LORA:DECAY_LORA + ICLR_LORA]
    xg = w[:, DECAY_LORA + ICLR_LORA:]
    padto = lambda t, n: jnp.pad(t, ((0, 0), (0, n - t.shape[1])))
    return jnp.concatenate([padto(xw, LANES), padto(xa, LANES), padto(xg, 2 * LANES)], axis=1)


def _ffn(x2d, norm_g, w_in, w_out, layer):
    xn = _rmsnorm(x2d, norm_g, BF16)
    hid, w_out_bf16 = _matmul_swiglu(xn, w_in, w_out, layer, BF16)
    return _matmul_residual(hid, w_out_bf16, x2d, 0.5)


def kernel(x, ffn1_norm, ffn1_w_in, ffn1_w_out, mix_norm, mix_w_in, conv_w, conv_b, conv_ln_g, conv_ln_b, rwkv_mu, rwkv_w0, rwkv_w_up, rwkv_a0, rwkv_a_up, rwkv_g_up, rwkv_k_k, rwkv_k_a, rwkv_r_k, rwkv_lnx_g, rwkv_lnx_b, sgu_ln_g, sgu_ln_b, sgu_w_s, sgu_b_s, mix_w_out, ffn2_norm, ffn2_w_in, ffn2_w_out, final_norm):
    b, s, d = x.shape
    depth = ffn1_norm.shape[0]
    c = conv_w.shape[2]
    o_lora = 5 * c
    o_sgu = o_lora + DECAY_LORA + ICLR_LORA + GATE_LORA
    w_mix_t = jnp.swapaxes(mix_w_in, 1, 2)

    x2d = x.reshape(b * s, d)
    for l in range(depth):
        x2d = _ffn(x2d, ffn1_norm[l], ffn1_w_in, ffn1_w_out, l)

        xn = _rmsnorm(x2d, mix_norm[l], BF16)
        mm = lambda wt, lyr, row0, n, dt: _matmul_nt(xn, wt, lyr, row0, n, dt).reshape(b, s, n)
        w_lora_t = _lora_slots(mix_w_in[l, :, o_lora:o_sgu]).T[None]
        h_conv_rkv = mm(w_mix_t, l, 0, 5 * c, F32)
        h_lora = mm(w_lora_t, 0, 0, w_lora_t.shape[1], F32)
        h_sgu = mm(w_mix_t, l, o_sgu, 2 * c, F32)
        h_sb = mm(w_mix_t, l, o_sgu + 2 * c, 3 * c, BF16)

        y_conv = _conv_mixer(h_conv_rkv, conv_w[l], conv_b[l], conv_ln_g[l], conv_ln_b[l])
        y_rwkv = _rwkv_mixer(h_conv_rkv, 2 * c, h_lora, rwkv_mu[l], rwkv_w0[l], rwkv_w_up[l], rwkv_a0[l], rwkv_a_up[l],
                             rwkv_g_up[l], rwkv_k_k[l], rwkv_k_a[l], rwkv_r_k[l], rwkv_lnx_g[l], rwkv_lnx_b[l])
        y_sgu = _sgu_mixer(h_sgu, sgu_ln_g[l], sgu_ln_b[l], sgu_w_s[l], sgu_b_s[l])
        y_sb = _sb_mixer(h_sb)
        groups = [t.reshape(b * s, c) for t in (y_conv, y_rwkv, y_sgu, y_sb)]
        x2d = _matmul_groups_residual(groups, mix_w_out, l, x2d)

        x2d = _ffn(x2d, ffn2_norm[l], ffn2_w_in, ffn2_w_out, l)
    return _rmsnorm(x2d, final_norm, F32).reshape(b, s, d)
```
